```python
import math
import jax, jax.numpy as jnp
from jax import lax
import numpy as np

D_MODEL = 1024
BATCH = 16
SEQ = 2048
DEPTH = 4
DEC_BATCH = 32
DEC_SEQ = 16
PAST_LEN = 2048

CHUNK = 64
RET_HEADS = 4
RET_DK = 128
RET_DV = 256
S5_GROUP = 16
S5_GROUPS = 32
S5_WIDTH = S5_GROUP * S5_GROUPS
S5_STATE = 64
D_FF = 4 * D_MODEL
PLE_DIM = 256
ROPE_BASE = 10000.0
EPS = 1e-6
Q_W = RET_HEADS * RET_DK
V_W = RET_HEADS * RET_DV
IN_W = 2 * Q_W + 2 * V_W + S5_WIDTH + 2 * D_MODEL
SPLIT_IDX = (Q_W, 2 * Q_W, 2 * Q_W + V_W, 2 * Q_W + 2 * V_W, 2 * Q_W + 2 * V_W + S5_WIDTH, 2 * Q_W + 2 * V_W + S5_WIDTH + D_MODEL)
LOG_DT_MIN = math.log(1e-3)
LOG_DT_MAX = math.log(1e-1)

kernel_name = 'retnet_s5_gated_hybrid_stream_step'


def rmsnorm(x, g):
    xf = x.astype(jnp.float32)
    y = xf * lax.rsqrt(jnp.mean(xf * xf, axis=-1, keepdims=True) + EPS)
    return (y * g.astype(jnp.float32)).astype(x.dtype)


def rope(x, pos):
    half = x.shape[-1] // 2
    inv = ROPE_BASE ** (-jnp.arange(half, dtype=jnp.float32) / half)
    ang = pos.astype(jnp.float32)[:, None] * inv[None, :]
    cos = jnp.cos(ang)[None, :, None, :]
    sin = jnp.sin(ang)[None, :, None, :]
    x1 = x[..., :half].astype(jnp.float32)
    x2 = x[..., half:].astype(jnp.float32)
    return jnp.concatenate([x1 * cos - x2 * sin, x1 * sin + x2 * cos], axis=-1).astype(x.dtype)


def retention(q, k, v, s0):
    bt, L, H, _ = q.shape
    dv = v.shape[-1]
    c = min(CHUNK, L)
    n = L // c
    lg = jnp.log1p(-(2.0 ** (-5.0 - jnp.arange(H, dtype=jnp.float32))))
    idx = jnp.arange(c, dtype=jnp.float32)
    diff = idx[:, None] - idx[None, :]
    inner_decay = jnp.where(diff[None] >= 0, jnp.exp(jnp.maximum(diff, 0.0)[None] * lg[:, None, None]), 0.0)
    q_decay = jnp.exp((idx + 1.0)[:, None] * lg[None, :])
    k_decay = jnp.exp((c - 1.0 - idx)[:, None] * lg[None, :])
    chunk_decay = jnp.exp(c * lg)

    def to_chunks(t):
        return t.astype(jnp.float32).reshape(bt, n, c, H, t.shape[-1]).transpose(1, 0, 2, 3, 4)

    def step(S, inp):
        qc, kc, vc = inp
        scores = jnp.einsum('bihd,bjhd->bhij', qc, kc) * inner_decay[None]
        o = jnp.einsum('bhij,bjhe->bihe', scores, vc) + jnp.einsum('bihd,bhde->bihe', qc * q_decay[None, :, :, None], S)
        S = chunk_decay[None, :, None, None] * S + jnp.einsum('bjhd,bjhe->bhde', kc * k_decay[None, :, :, None], vc)
        return S, o

    S, o = lax.scan(step, s0.astype(jnp.float32), (to_chunks(q), to_chunks(k), to_chunks(v)))
    o = o.transpose(1, 0, 2, 3, 4).reshape(bt, L, H, dv)
    return o, S


def s5_scan(u, x0_re, x0_im, a_re, a_im, log_dt, b_re, b_im, c_re, c_im, d_skip):
    bt, L, _ = u.shape
    f32 = jnp.float32
    uf = u.astype(f32)
    ug = uf.reshape(bt, L, S5_GROUPS, S5_GROUP)
    a_re = a_re.astype(f32)
    a_im = a_im.astype(f32)
    dt = jnp.exp(log_dt.astype(f32))[:, None]
    mag = jnp.exp(a_re * dt)
    lr = mag * jnp.cos(a_im * dt)
    li = mag * jnp.sin(a_im * dt)
    den = a_re * a_re + a_im * a_im
    fr = ((lr - 1.0) * a_re + li * a_im) / den
    fi = (li * a_re - (lr - 1.0) * a_im) / den
    b_re = b_re.astype(f32)
    b_im = b_im.astype(f32)
    bb_re = fr[..., None] * b_re - fi[..., None] * b_im
    bb_im = fr[..., None] * b_im + fi[..., None] * b_re
    br = jnp.einsum('blgc,gnc->blgn', ug, bb_re)
    bi = jnp.einsum('blgc,gnc->blgn', ug, bb_im)
    x0r = x0_re.astype(f32)
    x0i = x0_im.astype(f32)
    br = br.at[:, 0].add(lr * x0r - li * x0i)
    bi = bi.at[:, 0].add(lr * x0i + li * x0r)
    ar = jnp.broadcast_to(lr, (1, L, S5_GROUPS, S5_STATE))
    ai = jnp.broadcast_to(li, (1, L, S5_GROUPS, S5_STATE))

    def combine(e1, e2):
        ar1, ai1, br1, bi1 = e1
        ar2, ai2, br2, bi2 = e2
        return (ar2 * ar1 - ai2 * ai1,
                ar2 * ai1 + ai2 * ar1,
                ar2 * br1 - ai2 * bi1 + br2,
                ar2 * bi1 + ai2 * br1 + bi2)

    _, _, xr, xi = lax.associative_scan(combine, (ar, ai, br, bi), axis=1)
    y = jnp.einsum('blgn,gcn->blgc', xr, c_re.astype(f32)) - jnp.einsum('blgn,gcn->blgc', xi, c_im.astype(f32))
    y = y.reshape(bt, L, S5_WIDTH) + d_skip.astype(f32) * uf
    return y, xr[:, -1], xi[:, -1]


def trunk_layer(h, p_i, pos, s_ret0, s5_re0, s5_im0,
                g_mix, w_in, g_ret_head, w_ret_o, s5_a_re, s5_a_im, s5_log_dt, s5_b_re, s5_b_im,
                s5_c_re, s5_c_im, s5_d, w_glu_a, w_glu_b, w_out, g_mlp, w_up, w_down,
                g_ple, w_ple_gate, w_ple_proj):
    bt, L, _ = h.shape
    hn = rmsnorm(h, g_mix)
    z = hn @ w_in
    q, k, v, g, u, gate_a, gate_b = jnp.split(z, SPLIT_IDX, axis=-1)
    q = rope(q.reshape(bt, L, RET_HEADS, RET_DK), pos)
    k = rope(k.reshape(bt, L, RET_HEADS, RET_DK), pos) * (RET_DK ** -0.5)
    v = v.reshape(bt, L, RET_HEADS, RET_DV)
    o_ret, s_ret = retention(q, k, v, s_ret0)
    o_ret = rmsnorm(o_ret, g_ret_head.reshape(RET_HEADS, RET_DV)).reshape(bt, L, V_W).astype(h.dtype)
    branch_a = (jax.nn.silu(g) * o_ret) @ w_ret_o
    y_s5, s5r, s5i = s5_scan(u, s5_re0, s5_im0, s5_a_re, s5_a_im, s5_log_dt, s5_b_re, s5_b_im, s5_c_re, s5_c_im, s5_d)
    y_s5 = jax.nn.gelu(y_s5.astype(h.dtype))
    branch_b = (y_s5 @ w_glu_a) * jax.nn.sigmoid(y_s5 @ w_glu_b)
    h = h + (jax.nn.sigmoid(gate_a) * branch_a + jax.nn.sigmoid(gate_b) * branch_b) @ w_out
    h = h + jnp.square(jax.nn.relu(rmsnorm(h, g_mlp) @ w_up)) @ w_down
    h = h + jax.nn.sigmoid(rmsnorm(h, g_ple) @ w_ple_gate) * (p_i @ w_ple_proj)
    return h, s_ret, s5r, s5i


def run_trunk(x, p, pos, s_ret0, s5_re0, s5_im0, weights):
    h = x
    rets, s5rs, s5is = [], [], []
    for i in range(DEPTH):
        lw = [w[i] for w in weights]
        h, sr, s5r, s5i = trunk_layer(h, p[i], pos, s_ret0[i], s5_re0[i], s5_im0[i], *lw)
        rets.append(sr)
        s5rs.append(s5r)
        s5is.append(s5i)
    return h, jnp.stack(rets), jnp.stack(s5rs), jnp.stack(s5is)


def setup_inputs(seed: int = 0) -> dict:
    key = jax.random.key(seed)
    ks = jax.random.split(key, 32)
    f32 = jnp.float32

    def nrm(k, shape, scale):
        return scale * jax.random.normal(k, shape, f32)

    return {
        'x_prompt': nrm(ks[0], (BATCH, SEQ, D_MODEL), 1.0),
        'x_sample': nrm(ks[1], (DEC_BATCH, DEC_SEQ, D_MODEL), 1.0),
        'state_ret': nrm(ks[2], (DEPTH, DEC_BATCH, RET_HEADS, RET_DK, RET_DV), 0.1),
        'state_s5_re': nrm(ks[3], (DEPTH, DEC_BATCH, S5_GROUPS, S5_STATE), 0.1),
        'state_s5_im': nrm(ks[4], (DEPTH, DEC_BATCH, S5_GROUPS, S5_STATE), 0.1),
        'p_prompt': nrm(ks[5], (DEPTH, BATCH, SEQ, PLE_DIM), 1.0),
        'p_sample': nrm(ks[6], (DEPTH, DEC_BATCH, DEC_SEQ, PLE_DIM), 1.0),
        'g_mix': 1.0 + nrm(ks[7], (DEPTH, D_MODEL), 0.02),
        'w_in': nrm(ks[8], (DEPTH, D_MODEL, IN_W), D_MODEL ** -0.5),
        'g_ret_head': 1.0 + nrm(ks[9], (DEPTH, V_W), 0.02),
        'w_ret_o': nrm(ks[10], (DEPTH, V_W, D_MODEL), V_W ** -0.5),
        's5_a_re': -0.5 + nrm(ks[11], (DEPTH, S5_GROUPS, S5_STATE), 0.01),
        's5_a_im': jnp.pi * jnp.arange(S5_STATE, dtype=f32) + nrm(ks[12], (DEPTH, S5_GROUPS, S5_STATE), 0.01),
        's5_log_dt': jax.random.uniform(ks[13], (DEPTH, S5_GROUPS), f32, LOG_DT_MIN, LOG_DT_MAX),
        's5_b_re': nrm(ks[14], (DEPTH, S5_GROUPS, S5_STATE, S5_GROUP), (2 * S5_GROUP) ** -0.5),
        's5_b_im': nrm(ks[15], (DEPTH, S5_GROUPS, S5_STATE, S5_GROUP), (2 * S5_GROUP) ** -0.5),
        's5_c_re': nrm(ks[16], (DEPTH, S5_GROUPS, S5_GROUP, S5_STATE), S5_STATE ** -0.5),
        's5_c_im': nrm(ks[17], (DEPTH, S5_GROUPS, S5_GROUP, S5_STATE), S5_STATE ** -0.5),
        's5_d': nrm(ks[18], (DEPTH, S5_WIDTH), 1.0),
        'w_glu_a': nrm(ks[19], (DEPTH, S5_WIDTH, D_MODEL), S5_WIDTH ** -0.5),
        'w_glu_b': nrm(ks[20], (DEPTH, S5_WIDTH, D_MODEL), S5_WIDTH ** -0.5),
        'w_out': nrm(ks[21], (DEPTH, D_MODEL, D_MODEL), D_MODEL ** -0.5),
        'g_mlp': 1.0 + nrm(ks[22], (DEPTH, D_MODEL), 0.02),
        'w_up': nrm(ks[23], (DEPTH, D_MODEL, D_FF), D_MODEL ** -0.5),
        'w_down': nrm(ks[24], (DEPTH, D_FF, D_MODEL), D_FF ** -0.5),
        'g_ple': 1.0 + nrm(ks[25], (DEPTH, D_MODEL), 0.02),
        'w_ple_gate': nrm(ks[26], (DEPTH, D_MODEL, D_MODEL), D_MODEL ** -0.5),
        'w_ple_proj': nrm(ks[27], (DEPTH, PLE_DIM, D_MODEL), PLE_DIM ** -0.5),
        'g_final': 1.0 + nrm(ks[28], (D_MODEL,), 0.02),
    }


def reference(x_prompt, x_sample, state_ret, state_s5_re, state_s5_im, p_prompt, p_sample,
              g_mix, w_in, g_ret_head, w_ret_o, s5_a_re, s5_a_im, s5_log_dt, s5_b_re, s5_b_im,
              s5_c_re, s5_c_im, s5_d, w_glu_a, w_glu_b, w_out, g_mlp, w_up, w_down,
              g_ple, w_ple_gate, w_ple_proj, g_final):
    weights = (g_mix, w_in, g_ret_head, w_ret_o, s5_a_re, s5_a_im, s5_log_dt, s5_b_re, s5_b_im,
               s5_c_re, s5_c_im, s5_d, w_glu_a, w_glu_b, w_out, g_mlp, w_up, w_down,
               g_ple, w_ple_gate, w_ple_proj)
    bp = x_prompt.shape[0]
    pos_prompt = jnp.arange(x_prompt.shape[1])
    pos_sample = PAST_LEN + jnp.arange(x_sample.shape[1])
    zr = jnp.zeros((DEPTH, bp, RET_HEADS, RET_DK, RET_DV), jnp.float32)
    zs = jnp.zeros((DEPTH, bp, S5_GROUPS, S5_STATE), jnp.float32)
    h_p, ret_p, s5re_p, s5im_p = run_trunk(x_prompt, p_prompt, pos_prompt, zr, zs, zs, weights)
    h_s, ret_s, s5re_s, s5im_s = run_trunk(x_sample, p_sample, pos_sample, state_ret, state_s5_re, state_s5_im, weights)
    y_prompt = rmsnorm(h_p, g_final)
    y_sample = rmsnorm(h_s, g_final)
    return (y_prompt, y_sample, ret_p, s5re_p, s5im_p, ret_s, s5re_s, s5im_s)
```

```python
import functools
import math

import jax
import jax.numpy as jnp
import numpy as np
from jax import lax
from jax.experimental import pallas as pl
from jax.experimental.pallas import tpu as pltpu

F32 = jnp.float32
BF16 = jnp.bfloat16
HIGHEST = lax.Precision.HIGHEST

EPS = 1e-6
ROPE_BASE = 10000.0
PAST_LEN = 2048
RET_HEADS = 4
RET_DK = 128
RET_DV = 256
Q_W = RET_HEADS * RET_DK
V_W = RET_HEADS * RET_DV
S5_GROUP = 16
S5_GROUPS = 32
S5_WIDTH = S5_GROUP * S5_GROUPS
S5_STATE = 64
S5_CHUNK = 16
S5_COLS = S5_CHUNK * S5_GROUP
LANES = 128
SLOTS = LANES // S5_GROUP
VMEM_LIMIT = 56 * 1024 * 1024

RET_CHUNK = 256
TOKEN_BLOCK = 512
POST_TOKEN_BLOCK = 256


def _resident(shape):
    nd = len(shape)
    return pl.BlockSpec(shape, lambda *_: (0,) * nd, pipeline_mode=pl.Buffered(1))


def _sigmoid(x):
    return 1.0 / (1.0 + jnp.exp(-x))


def _rms(x, g):
    return (x * lax.rsqrt(jnp.mean(x * x, axis=-1, keepdims=True) + EPS)) * g


def _dot(a, b):
    return jnp.dot(a, b, preferred_element_type=F32)


def _in_proj_body(x_ref, gm_ref, w_ref, cos_ref, sin_ref, q_ref, k_ref, v_ref, g_ref, u_ref, ga_ref, gb_ref):
    hn = _rms(x_ref[...], gm_ref[...]).astype(BF16)
    cos = cos_ref[...]
    sin = sin_ref[...]

    def proj(lo, hi):
        return _dot(hn, w_ref[:, lo:hi])

    def rope(z, hd):
        xh = z[:, hd * RET_DK:(hd + 1) * RET_DK]
        return xh * cos + pltpu.roll(xh, RET_DK // 2, 1) * sin

    zq = proj(0, Q_W)
    zk = proj(Q_W, 2 * Q_W)
    for hd in range(RET_HEADS):
        sl = slice(hd * RET_DK, (hd + 1) * RET_DK)
        q_ref[:, sl] = rope(zq, hd).astype(BF16)
        k_ref[:, sl] = (rope(zk, hd) * (RET_DK ** -0.5)).astype(BF16)
    o = 2 * Q_W
    v_ref[...] = proj(o, o + V_W).astype(BF16)
    g_ref[...] = proj(o + V_W, o + 2 * V_W).astype(BF16)
    o += 2 * V_W
    zu = proj(o, o + S5_WIDTH)
    for vc in range(S5_WIDTH // LANES):
        u_ref[vc] = zu[:, vc * LANES:(vc + 1) * LANES]
    o += S5_WIDTH
    d = x_ref.shape[1]
    ga_ref[...] = proj(o, o + d).astype(BF16)
    gb_ref[...] = proj(o + d, o + 2 * d).astype(BF16)


def _in_proj(h, g_mix, w_in, cos_t, sin_t, seq_len):
    t, d = h.shape
    tm = min(TOKEN_BLOCK, t)
    assert t % tm == 0 and (seq_len % tm == 0 or tm % seq_len == 0)
    n_pos_blocks = cos_t.shape[0] // tm
    tok = lambda w: pl.BlockSpec((tm, w), lambda i: (i, 0))
    pos = pl.BlockSpec((tm, RET_DK), lambda i: (i % n_pos_blocks, 0))
    bf = lambda w: (tok(w), jax.ShapeDtypeStruct((t, w), BF16))
    n_uc = S5_WIDTH // LANES
    u_out = (pl.BlockSpec((n_uc, tm, LANES), lambda i: (0, i, 0)), jax.ShapeDtypeStruct((n_uc, t, LANES), F32))
    outs = (bf(Q_W), bf(Q_W), bf(V_W), bf(V_W), u_out, bf(d), bf(d))
    return pl.pallas_call(
        _in_proj_body,
        grid=(t // tm,),
        in_specs=[tok(d), _resident((1, d)), _resident(w_in.shape), pos, pos],
        out_specs=[o[0] for o in outs],
        out_shape=[o[1] for o in outs],
        compiler_params=pltpu.CompilerParams(dimension_semantics=("arbitrary",), vmem_limit_bytes=VMEM_LIMIT),
        name="in_proj",
    )(h, g_mix.reshape(1, d), w_in, cos_t, sin_t)


def _retention_body(q_ref, k_ref, v_ref, g_ref, s0_ref, gh_ref, o_ref, s_ref, *, chunk):
    @pl.when(pl.program_id(1) == 0)
    def _():
        s_ref[...] = s0_ref[...]

    c = chunk
    row_cc = lax.broadcasted_iota(jnp.int32, (c, c), 0)
    col_cc = lax.broadcasted_iota(jnp.int32, (c, c), 1)
    diff = (row_cc - col_cc).astype(F32)
    idx_v = lax.broadcasted_iota(jnp.int32, (c, RET_DV), 0).astype(F32)
    idx_k = lax.broadcasted_iota(jnp.int32, (c, RET_DK), 0).astype(F32)
    for hd in range(RET_HEADS):
        lg = math.log1p(-(2.0 ** (-5.0 - hd)))
        inner = jnp.where(diff >= 0.0, jnp.exp(jnp.maximum(diff, 0.0) * lg), 0.0)
        q_decay = jnp.exp((idx_v + 1.0) * lg)
        k_decay = jnp.exp((c - 1.0 - idx_k) * lg)
        q = q_ref[0, :, hd * RET_DK:(hd + 1) * RET_DK]
        k = k_ref[0, :, hd * RET_DK:(hd + 1) * RET_DK]
        v = v_ref[0, :, hd * RET_DV:(hd + 1) * RET_DV]
        s = s_ref[0, hd]
        scores = lax.dot_general(q, k, (((1,), (1,)), ((), ())), preferred_element_type=F32) * inner
        o = _dot(scores.astype(BF16), v) + q_decay * _dot(q, s.astype(BF16))
        kd = (k.astype(F32) * k_decay).astype(BF16)
        s_ref[0, hd] = math.exp(c * lg) * s + lax.dot_general(
            kd, v, (((0,), (0,)), ((), ())), preferred_element_type=F32)
        on = _rms(o, gh_ref[:, hd * RET_DV:(hd + 1) * RET_DV])
        gate = g_ref[0, :, hd * RET_DV:(hd + 1) * RET_DV].astype(F32)
        o_ref[0, :, hd * RET_DV:(hd + 1) * RET_DV] = (gate * _sigmoid(gate) * on).astype(BF16)


def _retention(q, k, v, g, s0, g_head):
    b, l, _ = q.shape
    c = min(RET_CHUNK, l)
    assert l % c == 0
    seq = lambda w: pl.BlockSpec((1, c, w), lambda i, j: (i, j, 0))
    st = pl.BlockSpec((1, RET_HEADS, RET_DK, RET_DV), lambda i, j: (i, 0, 0, 0))
    return pl.pallas_call(
        functools.partial(_retention_body, chunk=c),
        grid=(b, l // c),
        in_specs=[seq(Q_W), seq(Q_W), seq(V_W), seq(V_W), st, _resident((1, V_W))],
        out_specs=[seq(V_W), st],
        out_shape=[jax.ShapeDtypeStruct((b, l, V_W), BF16),
                   jax.ShapeDtypeStruct((b, RET_HEADS, RET_DK, RET_DV), F32)],
        compiler_params=pltpu.CompilerParams(dimension_semantics=("arbitrary", "arbitrary"),
                                             vmem_limit_bytes=VMEM_LIMIT),
        name="retention",
    )(q, k, v, g, s0, g_head.reshape(1, V_W))


def _s5_operators(a_re, a_im, log_dt, b_re, b_im, c_re, c_im, d_skip, levels):
    n_pos = S5_CHUNK
    dt = jnp.exp(log_dt)[:, None]
    mag = jnp.exp(a_re * dt)
    lr = mag * jnp.cos(a_im * dt)
    li = mag * jnp.sin(a_im * dt)
    den = a_re * a_re + a_im * a_im
    fr = ((lr - 1.0) * a_re + li * a_im) / den
    fi = (li * a_re - (lr - 1.0) * a_im) / den
    bb_re = fr[..., None] * b_re - fi[..., None] * b_im
    bb_im = fr[..., None] * b_im + fi[..., None] * b_re

    def lam_pow(kk):
        kk = kk.astype(F32)[:, None, None]
        m = jnp.exp(kk * (a_re * dt)[None])
        return m * jnp.cos(kk * (a_im * dt)[None]), m * jnp.sin(kk * (a_im * dt)[None])

    pr, pi = lam_pow(jnp.arange(n_pos + 1))
    cp_re = c_re[None] * pr[:n_pos, :, None, :] - c_im[None] * pi[:n_pos, :, None, :]
    cp_im = c_re[None] * pi[:n_pos, :, None, :] + c_im[None] * pr[:n_pos, :, None, :]
    kern = (jnp.einsum('lgon,gnc->gloc', cp_re, bb_re, precision=HIGHEST)
            - jnp.einsum('lgon,gnc->gloc', cp_im, bb_im, precision=HIGHEST))
    pos = jnp.arange(n_pos)
    lag = pos[None, :] - pos[:, None]
    m_nat = kern[:, jnp.maximum(lag, 0)]
    m_nat = jnp.where((lag >= 0)[None, :, :, None, None], m_nat, 0.0)
    m_nat = m_nat.transpose(0, 1, 4, 2, 3).reshape(S5_GROUPS, S5_COLS, S5_COLS)
    rr = pr[n_pos - 1 - pos]
    ri = pi[n_pos - 1 - pos]
    g_re = rr[..., None] * bb_re[None] - ri[..., None] * bb_im[None]
    g_im = rr[..., None] * bb_im[None] + ri[..., None] * bb_re[None]
    g_nat = jnp.concatenate([g_re, g_im], axis=2).transpose(1, 0, 3, 2).reshape(S5_GROUPS, S5_COLS, 2 * S5_STATE)
    hp_re = c_re[None] * pr[1:, :, None, :] - c_im[None] * pi[1:, :, None, :]
    hp_im = c_re[None] * pi[1:, :, None, :] + c_im[None] * pr[1:, :, None, :]
    h_nat = jnp.concatenate([hp_re, -hp_im], axis=3).transpose(1, 3, 0, 2).reshape(S5_GROUPS, 2 * S5_STATE, S5_COLS)
    sig = np.arange(S5_CHUNK)
    grp = np.arange(S5_GROUPS)
    t_of = SLOTS * (sig[None, :] // SLOTS) + (sig[None, :] % SLOTS - grp[:, None]) % SLOTS
    perm = (t_of[:, :, None] * S5_GROUP + np.arange(S5_GROUP)[None, None, :]).reshape(S5_GROUPS, S5_COLS)
    gi = grp[:, None, None]
    m_op = m_nat[gi, perm[:, :, None], perm[:, None, :]].astype(BF16)
    g_op = g_nat[gi, perm[:, :, None], np.arange(2 * S5_STATE)[None, None, :]].astype(BF16)
    h_op = h_nat[gi, np.arange(2 * S5_STATE)[None, :, None], perm[:, None, :]].astype(BF16)
    dr, di = lam_pow(n_pos * (2 ** jnp.arange(max(levels, 1))))
    d_a = jnp.concatenate([dr, dr], axis=-1).transpose(1, 0, 2)
    d_b = jnp.concatenate([-di, di], axis=-1).transpose(1, 0, 2)
    d_tile = jnp.tile(d_skip.reshape(S5_GROUPS, 1, S5_GROUP), (1, S5_CHUNK, 1)).reshape(S5_GROUPS, 1, S5_COLS)
    return m_op, g_op, h_op, d_a, d_b, d_tile


def _s5_body(u_ref, x0_ref, m_ref, g_ref, h_ref, da_ref, db_ref, dsk_ref, y_ref, xf_ref, v_scr, y_scr,
             *, rows, seg, levels):
    r = rows
    half = S5_STATE
    n_uc = S5_WIDTH // LANES
    slot =lax.broadcasted_iota(jnp.int32, (r, LANES), 1) // S5_GROUP
    masks = [slot == s for s in range(SLOTS)]

    rolled = []
    for t in range(S5_CHUNK):
        a = jnp.concatenate([u_ref[vc, pl.ds(t, r, stride=S5_CHUNK), :] for vc in range(n_uc)], axis=1)
        j = t % SLOTS
        rolled.append(pltpu.roll(a, S5_GROUP * j, 1) if j else a)
    for g in range(S5_GROUPS):
        for col in range(S5_CHUNK // SLOTS):
            acc = None
            for tt in range(SLOTS):
                p = (g + tt) % S5_GROUPS
                src = rolled[col * SLOTS + tt][:, (p // SLOTS) * LANES:(p // SLOTS + 1) * LANES]
                acc = src if acc is None else jnp.where(masks[p % SLOTS], src, acc)
            v_scr[g, :, col * LANES:(col + 1) * LANES] = acc

    row = lax.broadcasted_iota(jnp.int32, (r, 2 * half), 0)

    def cmul(x, a, b):
        return x * a + pltpu.roll(x, half, 1) * b

    def group_step(g, carry):
        vf = v_scr[g]
        vb = vf.astype(BF16)
        y_intra = _dot(vb, m_ref[g])
        w = _dot(vb, g_ref[g])
        x0 = x0_ref[0, g]
        da = da_ref[g]
        db = db_ref[g]
        inj = cmul(x0, da[0:1], db[0:1])
        if seg == 1:
            w = w + inj
            x_in = x0
        else:
            w = w + jnp.where(row == 0, inj, 0.0)
            for kk in range(levels):
                s = 1 << kk
                sh = jnp.where(row >= s, pltpu.roll(w, s, 0), 0.0)
                w = w + cmul(sh, da[kk:kk + 1], db[kk:kk + 1])
            x_in = jnp.where(row == 0, x0, pltpu.roll(w, 1, 0))
        xf_ref[0, g] = w if seg == 1 else w[r - 1:r, :]
        y_scr[g] = y_intra + _dot(x_in.astype(BF16), h_ref[g]) + vf * dsk_ref[g]
        return carry

    lax.fori_loop(0, S5_GROUPS, group_step, 0)

    for t in range(S5_CHUNK):
        col, j = divmod(t, SLOTS)
        cols = []
        for vc in range(n_uc):
            acc = None
            for gg in range(SLOTS):
                p = vc * SLOTS + gg
                src = y_scr[(p - j) % S5_GROUPS, :, col * LANES:(col + 1) * LANES]
                acc = src if acc is None else jnp.where(masks[gg], src, acc)
            cols.append(acc)
        yt = jnp.concatenate(cols, axis=1)
        if j:
            yt = pltpu.roll(yt, S5_WIDTH - S5_GROUP * j, 1)
        for vc in range(n_uc):
            y_ref[vc, pl.ds(t, r, stride=S5_CHUNK), :] = yt[:, vc * LANES:(vc + 1) * LANES]


def _s5(u, x0, ops, seg_is_row):
    n_uc = u.shape[0]
    nb = x0.shape[0]
    n_tok = u.shape[1] // nb
    rows = n_tok // S5_CHUNK
    seg = 1 if seg_is_row else rows
    nseq = rows // seg
    levels = 0 if seg == 1 else (rows - 1).bit_length()
    m_op, g_op, h_op, d_a, d_b, d_tile = ops
    tok = pl.BlockSpec((n_uc, n_tok, LANES), lambda i: (0, i, 0))
    st = pl.BlockSpec((1, S5_GROUPS, nseq, 2 * S5_STATE), lambda i: (i, 0, 0, 0))
    return pl.pallas_call(
        functools.partial(_s5_body, rows=rows, seg=seg, levels=levels),
        grid=(nb,),
        in_specs=[tok, st, _resident(m_op.shape), _resident(g_op.shape), _resident(h_op.shape),
                  _resident(d_a.shape), _resident(d_b.shape), _resident(d_tile.shape)],
        out_specs=[tok, st],
        out_shape=[jax.ShapeDtypeStruct(u.shape, F32),
                   jax.ShapeDtypeStruct((nb, S5_GROUPS, nseq, 2 * S5_STATE), F32)],
        scratch_shapes=[pltpu.VMEM((S5_GROUPS, rows, S5_COLS), F32),
                        pltpu.VMEM((S5_GROUPS, rows, S5_COLS), F32)],
        compiler_params=pltpu.CompilerParams(dimension_semantics=("arbitrary",), vmem_limit_bytes=VMEM_LIMIT),
        name="s5",
    )(u, x0, m_op, g_op, h_op, d_a, d_b, d_tile)


def _gelu_tanh(x):
    return 0.5 * x * (1.0 + jnp.tanh(math.sqrt(2.0 / math.pi) * (x + 0.044715 * (x * x * x))))


def _post_body(h_ref, o_ref, y_ref, ga_ref, gb_ref, p_ref, wro_ref, wglu_ref, wout_ref, gmlp_ref, wup_ref,
               wdown_ref, gple_ref, wpg_ref, wpp_ref, gfin_ref, out_ref, *, final):
    d = h_ref.shape[1]
    h = h_ref[...]
    branch_a = _dot(o_ref[...], wro_ref[...])
    y = jnp.concatenate([y_ref[vc] for vc in range(y_ref.shape[0])], axis=1)
    glu = _dot(_gelu_tanh(y).astype(BF16), wglu_ref[...])
    branch_b = glu[:, :d] * _sigmoid(glu[:, d:])
    merged = (_sigmoid(ga_ref[...].astype(F32)) * branch_a + _sigmoid(gb_ref[...].astype(F32)) * branch_b)
    h = h + _dot(merged.astype(BF16), wout_ref[...])
    up = _dot(_rms(h, gmlp_ref[...]).astype(BF16), wup_ref[...])
    h = h + _dot(jnp.square(jnp.maximum(up, 0.0)).astype(BF16), wdown_ref[...])
    gate = _sigmoid(_dot(_rms(h, gple_ref[...]).astype(BF16), wpg_ref[...]))
    h = h + gate * _dot(p_ref[...].astype(BF16), wpp_ref[...])
    if final:
        h = _rms(h, gfin_ref[...])
    out_ref[...] = h


def _post(h, o, y, ga, gb, p, lw, g_final, final):
    t, d = h.shape
    tm = min(POST_TOKEN_BLOCK, t)
    assert t % tm == 0
    tok = lambda w: pl.BlockSpec((tm, w), lambda i: (i, 0))
    weights = (lw['w_ret_o'], lw['w_glu'], lw['w_out'], lw['g_mlp'], lw['w_up'], lw['w_down'],
               lw['g_ple'], lw['w_ple_gate'], lw['w_ple_proj'], g_final)
    return pl.pallas_call(
        functools.partial(_post_body, final=final),
        grid=(t // tm,),
        in_specs=[tok(d), tok(V_W), pl.BlockSpec((y.shape[0], tm, LANES), lambda i: (0, i, 0)), tok(d), tok(d),
                  tok(p.shape[1])]
                 + [_resident(w.shape) for w in weights],
        out_specs=tok(d),
        out_shape=jax.ShapeDtypeStruct((t, d), F32),
        compiler_params=pltpu.CompilerParams(dimension_semantics=("arbitrary",), vmem_limit_bytes=VMEM_LIMIT),
        name="post",
    )(h, o, y, ga, gb, p, *weights)


def _rope_tables(pos, n_rows):
    half = RET_DK // 2
    inv = ROPE_BASE ** (-jnp.arange(half, dtype=F32) / half)
    ang = pos.astype(F32)[:, None] * inv[None, :]
    cos = jnp.cos(ang)
    sin = jnp.sin(ang)
    reps = max(1, n_rows // pos.shape[0])
    return (jnp.tile(jnp.concatenate([cos, cos], axis=1), (reps, 1)),
            jnp.tile(jnp.concatenate([-sin, sin], axis=1), (reps, 1)))


def _run_trunk(x, p, pos, s_ret0, s5_re0, s5_im0, layers, g_final):
    b, l, d = x.shape
    t = b * l
    depth = len(layers)
    cos_t, sin_t = _rope_tables(pos, min(TOKEN_BLOCK, t))
    seg_is_row = l == S5_CHUNK
    assert seg_is_row or l % (8 * S5_CHUNK) == 0
    levels = 0 if seg_is_row else (l // S5_CHUNK - 1).bit_length()
    h = x.reshape(t, d)
    rets, s5s = [], []
    for i, lw in enumerate(layers):
        q, k, v, g, u, ga, gb = _in_proj(h, lw['g_mix'], lw['w_in'], cos_t, sin_t, l)
        o, s_ret = _retention(q.reshape(b, l, Q_W), k.reshape(b, l, Q_W), v.reshape(b, l, V_W),
                              g.reshape(b, l, V_W), s_ret0[i], lw['g_ret_head'])
        x0 = jnp.concatenate([s5_re0[i], s5_im0[i]], axis=-1)
        ops = _s5_operators(lw['s5_a_re'], lw['s5_a_im'], lw['s5_log_dt'], lw['s5_b_re'], lw['s5_b_im'],
                            lw['s5_c_re'], lw['s5_c_im'], lw['s5_d'], levels)
        if seg_is_row:
            y, xf = _s5(u, x0.transpose(1, 0, 2)[None], ops, True)
            xf = xf[0].transpose(1, 0, 2)
        else:
            y, xf = _s5(u, x0[:, :, None, :], ops, False)
            xf = xf[:, :, 0, :]
        h = _post(h, o.reshape(t, V_W), y, ga, gb, p[i].reshape(t, -1), lw, g_final, final=(i == depth - 1))
        rets.append(s_ret)
        s5s.append(xf)
    s5s = jnp.stack(s5s)
    return h.reshape(b, l, d), jnp.stack(rets), s5s[..., :S5_STATE], s5s[..., S5_STATE:]


def _layer_weights(i, g_mix, w_in, g_ret_head, w_ret_o, s5_a_re, s5_a_im, s5_log_dt, s5_b_re, s5_b_im,
                   s5_c_re, s5_c_im, s5_d, w_glu_a, w_glu_b, w_out, g_mlp, w_up, w_down, g_ple, w_ple_gate,
                   w_ple_proj):
    d = w_out.shape[-1]
    return {
        'g_mix': g_mix[i], 'w_in': w_in[i].astype(BF16), 'g_ret_head': g_ret_head[i],
        'w_ret_o': w_ret_o[i].astype(BF16),
        's5_a_re': s5_a_re[i], 's5_a_im': s5_a_im[i], 's5_log_dt': s5_log_dt[i], 's5_b_re': s5_b_re[i],
        's5_b_im': s5_b_im[i], 's5_c_re': s5_c_re[i], 's5_c_im': s5_c_im[i], 's5_d': s5_d[i],
        'w_glu': jnp.concatenate([w_glu_a[i], w_glu_b[i]], axis=1).astype(BF16),
        'w_out': w_out[i].astype(BF16), 'g_mlp': g_mlp[i].reshape(1, d), 'w_up': w_up[i].astype(BF16),
        'w_down': w_down[i].astype(BF16), 'g_ple': g_ple[i].reshape(1, d),
        'w_ple_gate': w_ple_gate[i].astype(BF16), 'w_ple_proj': w_ple_proj[i].astype(BF16),
    }


def kernel(x_prompt, x_sample, state_ret, state_s5_re, state_s5_im, p_prompt, p_sample, g_mix, w_in, g_ret_head, w_ret_o, s5_a_re, s5_a_im, s5_log_dt, s5_b_re, s5_b_im, s5_c_re, s5_c_im, s5_d, w_glu_a, w_glu_b, w_out, g_mlp, w_up, w_down, g_ple, w_ple_gate, w_ple_proj, g_final):
    depth = w_in.shape[0]
    d = x_prompt.shape[-1]
    layers = [_layer_weights(i, g_mix, w_in, g_ret_head, w_ret_o, s5_a_re, s5_a_im, s5_log_dt, s5_b_re, s5_b_im,
                             s5_c_re, s5_c_im, s5_d, w_glu_a, w_glu_b, w_out, g_mlp, w_up, w_down, g_ple,
                             w_ple_gate, w_ple_proj) for i in range(depth)]
    g_fin = g_final.reshape(1, d)
    bp = x_prompt.shape[0]
    pos_prompt = jnp.arange(x_prompt.shape[1])
    pos_sample = PAST_LEN + jnp.arange(x_sample.shape[1])
    zr = jnp.zeros((depth, bp, RET_HEADS, RET_DK, RET_DV), F32)
    zs = jnp.zeros((depth, bp, S5_GROUPS, S5_STATE), F32)
    y_p, ret_p, s5re_p, s5im_p = _run_trunk(x_prompt, p_prompt, pos_prompt, zr, zs, zs, layers, g_fin)
    y_s, ret_s, s5re_s, s5im_s = _run_trunk(x_sample, p_sample, pos_sample, state_ret, state_s5_re, state_s5_im,
                                            layers, g_fin)
    return (y_p, y_s, ret_p, s5re_p, s5im_p, ret_s, s5re_s, s5im_s)
```

```python
import functools
import math

import jax
import jax.numpy as jnp
import numpy as np
from jax import lax
from jax.experimental import pallas as pl
from jax.experimental.pallas import tpu as pltpu

F32 = jnp.float32
BF16 = jnp.bfloat16
HIGHEST = lax.Precision.HIGHEST

EPS = 1e-6
ROPE_BASE = 10000.0
PAST_LEN = 2048
RET_HEADS = 4
RET_DK = 128
RET_DV = 256
Q_W = RET_HEADS * RET_DK
V_W = RET_HEADS * RET_DV
S5_GROUP = 16
S5_GROUPS = 32
S5_WIDTH = S5_GROUP * S5_GROUPS
S5_STATE = 64
S5_CHUNK = 16
S5_COLS = S5_CHUNK * S5_GROUP
LANES = 128
SLOTS = LANES // S5_GROUP
S5_LANE_CHUNKS = S5_WIDTH // LANES
S5_FOLD_ROWS = 32
S5_GROUP_UNROLL = 4
VMEM_LIMIT = 56 * 1024 * 1024

RET_CHUNK = 256
RET_SHORT_SEQS = 8
TOKEN_BLOCK = 512
POST_TOKEN_BLOCK = 256


def _resident(shape):
    nd = len(shape)
    return pl.BlockSpec(shape, lambda *_: (0,) * nd, pipeline_mode=pl.Buffered(1))


def _layer_resident(shape, layer):
    nd = len(shape)
    return pl.BlockSpec((None,) + tuple(shape[1:]), lambda *_: (layer,) + (0,) * (nd - 1),
                        pipeline_mode=pl.Buffered(1))


def _sigmoid(x):
    return 1.0 / (1.0 + jnp.exp(-x))


def _rms(x, g):
    return (x * lax.rsqrt(jnp.mean(x * x, axis=-1, keepdims=True) + EPS)) * g


def _dot(a, b):
    return jnp.dot(a, b, preferred_element_type=F32)


def _in_proj_body(x_ref, gm_ref, w_ref, cos_ref, sin_ref, q_ref, k_ref, v_ref, g_ref, u_ref, ga_ref, gb_ref):
    hn = _rms(x_ref[...], gm_ref[...]).astype(BF16)
    cos = cos_ref[...]
    sin = sin_ref[...]

    def proj(lo, hi):
        return _dot(hn, w_ref[:, lo:hi])

    def rope(z, hd):
        xh = z[:, hd * RET_DK:(hd + 1) * RET_DK]
        return xh * cos + pltpu.roll(xh, RET_DK // 2, 1) * sin

    zq = proj(0, Q_W)
    zk = proj(Q_W, 2 * Q_W)
    for hd in range(RET_HEADS):
        sl = slice(hd * RET_DK, (hd + 1) * RET_DK)
        q_ref[:, sl] = rope(zq, hd).astype(BF16)
        k_ref[:, sl] = (rope(zk, hd) * (RET_DK ** -0.5)).astype(BF16)
    o = 2 * Q_W
    v_ref[...] = proj(o, o + V_W).astype(BF16)
    g_ref[...] = proj(o + V_W, o + 2 * V_W).astype(BF16)
    o += 2 * V_W
    zu = proj(o, o + S5_WIDTH)
    for vc in range(S5_LANE_CHUNKS):
        u_ref[vc] = zu[:, vc * LANES:(vc + 1) * LANES]
    o += S5_WIDTH
    d = x_ref.shape[1]
    ga_ref[...] = proj(o, o + d).astype(BF16)
    gb_ref[...] = proj(o + d, o + 2 * d).astype(BF16)


def _in_proj(h, layer, w, cos_t, sin_t, seq_len):
    t, d = h.shape
    tm = min(TOKEN_BLOCK, t)
    assert t % tm == 0 and (seq_len % tm == 0 or tm % seq_len == 0)
    n_pos_blocks = cos_t.shape[0] // tm
    tok = lambda wd: pl.BlockSpec((tm, wd), lambda i: (i, 0))
    pos = pl.BlockSpec((tm, RET_DK), lambda i: (i % n_pos_blocks, 0))
    bf = lambda wd: (tok(wd), jax.ShapeDtypeStruct((t, wd), BF16))
    u_out = (pl.BlockSpec((S5_LANE_CHUNKS, tm, LANES), lambda i: (0, i, 0)),
             jax.ShapeDtypeStruct((S5_LANE_CHUNKS, t, LANES), F32))
    outs = (bf(Q_W), bf(Q_W), bf(V_W), bf(V_W), u_out, bf(d), bf(d))
    return pl.pallas_call(
        _in_proj_body,
        grid=(t // tm,),
        in_specs=[tok(d), _layer_resident(w['g_mix'].shape, layer), _layer_resident(w['w_in'].shape, layer),
                  pos, pos],
        out_specs=[o[0] for o in outs],
        out_shape=[o[1] for o in outs],
        compiler_params=pltpu.CompilerParams(dimension_semantics=("arbitrary",), vmem_limit_bytes=VMEM_LIMIT),
        name="in_proj",
    )(h, w['g_mix'], w['w_in'], cos_t, sin_t)


def _retention_body(*refs, chunk, n_seq, zero_init):
    if zero_init:
        q_ref, k_ref, v_ref, g_ref, gh_ref, o_ref, s_ref, inner_scr, qd_scr, kd_scr = refs
    else:
        q_ref, k_ref, v_ref, g_ref, s0_ref, gh_ref, o_ref, s_ref, inner_scr, qd_scr, kd_scr = refs
    c = chunk

    @pl.when(pl.program_id(1) == 0)
    def _():
        s_ref[...] = jnp.zeros(s_ref.shape, F32) if zero_init else s0_ref[...]
        row_cc = lax.broadcasted_iota(jnp.int32, (c, c), 0)
        col_cc = lax.broadcasted_iota(jnp.int32, (c, c), 1)
        diff = (row_cc - col_cc).astype(F32)
        idx_v = lax.broadcasted_iota(jnp.int32, (c, RET_DV), 0).astype(F32)
        idx_k = lax.broadcasted_iota(jnp.int32, (c, RET_DK), 0).astype(F32)
        for hd in range(RET_HEADS):
            lg = math.log1p(-(2.0 ** (-5.0 - hd)))
            inner_scr[hd] = jnp.where(diff >= 0.0, jnp.exp(jnp.maximum(diff, 0.0) * lg), 0.0)
            qd_scr[hd] = jnp.exp((idx_v + 1.0) * lg)
            kd_scr[hd] = jnp.exp((c - 1.0 - idx_k) * lg)

    for sq in range(n_seq):
        for hd in range(RET_HEADS):
            lg = math.log1p(-(2.0 ** (-5.0 - hd)))
            q = q_ref[sq, :, hd * RET_DK:(hd + 1) * RET_DK]
            k = k_ref[sq, :, hd * RET_DK:(hd + 1) * RET_DK]
            v = v_ref[sq, :, hd * RET_DV:(hd + 1) * RET_DV]
            s = s_ref[sq, hd]
            scores = lax.dot_general(q, k, (((1,), (1,)), ((), ())), preferred_element_type=F32) * inner_scr[hd]
            o = _dot(scores.astype(BF16), v) + qd_scr[hd] * _dot(q, s.astype(BF16))
            kd = (k.astype(F32) * kd_scr[hd]).astype(BF16)
            s_ref[sq, hd] = math.exp(c * lg) * s + lax.dot_general(
                kd, v, (((0,), (0,)), ((), ())), preferred_element_type=F32)
            on = _rms(o, gh_ref[:, hd * RET_DV:(hd + 1) * RET_DV])
            gate = g_ref[sq, :, hd * RET_DV:(hd + 1) * RET_DV].astype(F32)
            o_ref[sq, :, hd * RET_DV:(hd + 1) * RET_DV] = (gate * _sigmoid(gate) * on).astype(BF16)


def _retention(q, k, v, g, s0, layer, w):
    b, l, _ = q.shape
    c = min(RET_CHUNK, l)
    assert l % c == 0
    n_seq = RET_SHORT_SEQS if (l == c and b % RET_SHORT_SEQS == 0) else 1
    seq = lambda wd: pl.BlockSpec((n_seq, c, wd), lambda i, j: (i, j, 0))
    st = pl.BlockSpec((n_seq, RET_HEADS, RET_DK, RET_DV), lambda i, j: (i, 0, 0, 0))
    st_in = pl.BlockSpec((None, n_seq, RET_HEADS, RET_DK, RET_DV), lambda i, j: (layer, i, 0, 0, 0))
    zero_init = s0 is None
    return pl.pallas_call(
        functools.partial(_retention_body, chunk=c, n_seq=n_seq, zero_init=zero_init),
        grid=(b // n_seq, l // c),
        in_specs=[seq(Q_W), seq(Q_W), seq(V_W), seq(V_W)] + ([] if zero_init else [st_in])
                 + [_layer_resident(w['g_ret_head'].shape, layer)],
        out_specs=[seq(V_W), st],
        out_shape=[jax.ShapeDtypeStruct((b, l, V_W), BF16),
                   jax.ShapeDtypeStruct((b, RET_HEADS, RET_DK, RET_DV), F32)],
        scratch_shapes=[pltpu.VMEM((RET_HEADS, c, c), F32), pltpu.VMEM((RET_HEADS, c, RET_DV), F32),
                        pltpu.VMEM((RET_HEADS, c, RET_DK), F32)],
        compiler_params=pltpu.CompilerParams(dimension_semantics=("arbitrary", "arbitrary"),
                                             vmem_limit_bytes=VMEM_LIMIT),
        name="retention",
    )(q, k, v, g, *([] if zero_init else [s0]), w['g_ret_head'])


def _s5_fold_positions():
    sig = np.arange(S5_CHUNK)
    grp = np.arange(S5_GROUPS)
    return SLOTS * (sig[None, :] // SLOTS) + (sig[None, :] % SLOTS - grp[:, None]) % SLOTS


def _s5_operators(a_re, a_im, log_dt, b_re, b_im, c_re, c_im, d_skip, levels):
    n_pos = S5_CHUNK
    dt = jnp.exp(log_dt)[..., None]
    mag = jnp.exp(a_re * dt)
    lr = mag * jnp.cos(a_im * dt)
    li = mag * jnp.sin(a_im * dt)
    den = a_re * a_re + a_im * a_im
    fr = ((lr - 1.0) * a_re + li * a_im) / den
    fi = (li * a_re - (lr - 1.0) * a_im) / den
    bb_re = fr[..., None] * b_re - fi[..., None] * b_im
    bb_im = fr[..., None] * b_im + fi[..., None] * b_re

    def lam_pow(kk):
        kk = jnp.asarray(kk, F32)[:, None, None, None]
        m = jnp.exp(kk * (a_re * dt)[None])
        return m * jnp.cos(kk * (a_im * dt)[None]), m * jnp.sin(kk * (a_im * dt)[None])

    pr, pi = lam_pow(np.arange(n_pos + 1))
    ein = functools.partial(jnp.einsum, precision=HIGHEST)
    cp_re = c_re[None] * pr[:n_pos, :, :, None, :] - c_im[None] * pi[:n_pos, :, :, None, :]
    cp_im = c_re[None] * pi[:n_pos, :, :, None, :] + c_im[None] * pr[:n_pos, :, :, None, :]
    kern = ein('klgon,lgnc->lgkoc', cp_re, bb_re) - ein('klgon,lgnc->lgkoc', cp_im, bb_im)
    t_of = _s5_fold_positions()
    lag = t_of[:, None, :] - t_of[:, :, None]
    toep = (lag[..., None] == np.arange(n_pos)).astype(np.float32)
    n_l = a_re.shape[0]
    m_op = ein('gstk,lgkoc->lgscto', toep, kern).reshape(n_l, S5_GROUPS, S5_COLS, S5_COLS)
    sel_g = ((n_pos - 1 - t_of)[..., None] == np.arange(n_pos + 1)).astype(np.float32)
    rr = ein('gsk,klgn->lgsn', sel_g, pr)[..., None]
    ri = ein('gsk,klgn->lgsn', sel_g, pi)[..., None]
    g_re = rr * bb_re[:, :, None] - ri * bb_im[:, :, None]
    g_im = rr * bb_im[:, :, None] + ri * bb_re[:, :, None]
    g_op = jnp.concatenate([g_re, g_im], axis=3).transpose(0, 1, 2, 4, 3).reshape(
        n_l, S5_GROUPS, S5_COLS, 2 * S5_STATE)
    sel_h = ((t_of + 1)[..., None] == np.arange(n_pos + 1)).astype(np.float32)
    hr = ein('gtk,klgn->lgtn', sel_h, pr)[:, :, :, None, :]
    hi = ein('gtk,klgn->lgtn', sel_h, pi)[:, :, :, None, :]
    hp_re = c_re[:, :, None] * hr - c_im[:, :, None] * hi
    hp_im = c_re[:, :, None] * hi + c_im[:, :, None] * hr
    h_op = jnp.concatenate([hp_re, -hp_im], axis=-1).transpose(0, 1, 4, 2, 3).reshape(
        n_l, S5_GROUPS, 2 * S5_STATE, S5_COLS)
    dr, di = lam_pow(n_pos * (2 ** np.arange(max(levels, 1))))
    d_a = jnp.concatenate([dr, dr], axis=-1).transpose(1, 2, 0, 3)
    d_b = jnp.concatenate([-di, di], axis=-1).transpose(1, 2, 0, 3)
    d_tile = jnp.tile(d_skip.reshape(n_l, S5_GROUPS, 1, S5_GROUP), (1, 1, S5_CHUNK, 1)).reshape(
        n_l, S5_GROUPS, 1, S5_COLS)
    return m_op.astype(BF16), g_op.astype(BF16), h_op.astype(BF16), d_a, d_b, d_tile


def _s5_body(u_ref, x0_ref, m_ref, g_ref, h_ref, da_ref, db_ref, dsk_ref, y_ref, xf_ref, v_scr, y_scr,
             *, rows, seg, levels):
    r = rows
    half = S5_STATE
    rp = min(S5_FOLD_ROWS, r)
    slot = lax.broadcasted_iota(jnp.int32, (rp, LANES), 1) // S5_GROUP
    masks = [slot == s for s in range(SLOTS)]
    n_col = S5_CHUNK // SLOTS

    for vc in range(S5_LANE_CHUNKS):
        for col in range(n_col):
            for r0 in range(0, r, rp):
                srcs = []
                for tt in range(SLOTS):
                    a = u_ref[vc, pl.ds(col * SLOTS + tt + S5_CHUNK * r0, rp, stride=S5_CHUNK), :]
                    srcs.append(pltpu.roll(a, S5_GROUP * tt, 1) if tt else a)
                for kk in range(SLOTS):
                    acc = srcs[0]
                    for tt in range(1, SLOTS):
                        acc = jnp.where(masks[(kk + tt) % SLOTS], srcs[tt], acc)
                    v_scr[vc * SLOTS + kk, r0:r0 + rp, col * LANES:(col + 1) * LANES] = acc

    row = lax.broadcasted_iota(jnp.int32, (r, 2 * half), 0)

    def cmul(x, a, b):
        return x * a + pltpu.roll(x, half, 1) * b

    def group_step(g, carry):
        vf = v_scr[g]
        vb = vf.astype(BF16)
        y_intra = _dot(vb, m_ref[g])
        w = _dot(vb, g_ref[g])
        x0 = x0_ref[0, g]
        da = da_ref[g]
        db = db_ref[g]
        inj = cmul(x0, da[0:1], db[0:1])
        if seg == 1:
            w = w + inj
            x_in = x0
        else:
            w = w + jnp.where(row == 0, inj, 0.0)
            for kk in range(levels):
                s = 1 << kk
                sh = jnp.where(row >= s, pltpu.roll(w, s, 0), 0.0)
                w = w + cmul(sh, da[kk:kk + 1], db[kk:kk + 1])
            x_in = jnp.where(row == 0, x0, pltpu.roll(w, 1, 0))
        xf_ref[0, g] = w if seg == 1 else w[r - 1:r, :]
        y_scr[g] = y_intra + _dot(x_in.astype(BF16), h_ref[g]) + vf * dsk_ref[g]
        return carry

    lax.fori_loop(0, S5_GROUPS, group_step, 0, unroll=S5_GROUP_UNROLL)

    for vc in range(S5_LANE_CHUNKS):
        for col in range(n_col):
            for r0 in range(0, r, rp):
                ys = [y_scr[vc * SLOTS + kk, r0:r0 + rp, col * LANES:(col + 1) * LANES] for kk in range(SLOTS)]
                for tt in range(SLOTS):
                    acc = ys[(-tt) % SLOTS]
                    for sg in range(1, SLOTS):
                        acc = jnp.where(masks[sg], ys[(sg - tt) % SLOTS], acc)
                    if tt:
                        acc = pltpu.roll(acc, LANES - S5_GROUP * tt, 1)
                    y_ref[vc, pl.ds(col * SLOTS + tt + S5_CHUNK * r0, rp, stride=S5_CHUNK), :] = acc


def _s5(u, x0, ops, layer, seg_is_row):
    nb = x0.shape[0]
    n_tok = u.shape[1] // nb
    rows = n_tok // S5_CHUNK
    seg = 1 if seg_is_row else rows
    nseq = rows // seg
    levels = 0 if seg == 1 else (rows - 1).bit_length()
    assert rows % min(S5_FOLD_ROWS, rows) == 0 and ops[3].shape[2] >= max(levels, 1)
    tok = pl.BlockSpec((S5_LANE_CHUNKS, n_tok, LANES), lambda i: (0, i, 0))
    st = pl.BlockSpec((1, S5_GROUPS, nseq, 2 * S5_STATE), lambda i: (i, 0, 0, 0))
    return pl.pallas_call(
        functools.partial(_s5_body, rows=rows, seg=seg, levels=levels),
        grid=(nb,),
        in_specs=[tok, st] + [_layer_resident(o.shape, layer) for o in ops],
        out_specs=[tok, st],
        out_shape=[jax.ShapeDtypeStruct(u.shape, F32),
                   jax.ShapeDtypeStruct((nb, S5_GROUPS, nseq, 2 * S5_STATE), F32)],
        scratch_shapes=[pltpu.VMEM((S5_GROUPS, rows, S5_COLS), F32),
                        pltpu.VMEM((S5_GROUPS, rows, S5_COLS), F32)],
        compiler_params=pltpu.CompilerParams(dimension_semantics=("arbitrary",), vmem_limit_bytes=VMEM_LIMIT),
        name="s5",
    )(u, x0, *ops)


def _gelu_tanh(x):
    return 0.5 * x * (1.0 + jnp.tanh(math.sqrt(2.0 / math.pi) * (x + 0.044715 * (x * x * x))))


def _post_body(h_ref, o_ref, y_ref, ga_ref, gb_ref, p_ref, wro_ref, wglu_ref, wout_ref, gmlp_ref, wup_ref,
               wdown_ref, gple_ref, wpg_ref, wpp_ref, gfin_ref, out_ref, *, final):
    d = h_ref.shape[1]
    h = h_ref[...]
    branch_a = _dot(o_ref[...], wro_ref[...])
    y = jnp.concatenate([y_ref[vc] for vc in range(y_ref.shape[0])], axis=1)
    glu = _dot(_gelu_tanh(y).astype(BF16), wglu_ref[...])
    branch_b = glu[:, :d] * _sigmoid(glu[:, d:])
    merged = (_sigmoid(ga_ref[...].astype(F32)) * branch_a + _sigmoid(gb_ref[...].astype(F32)) * branch_b)
    h = h + _dot(merged.astype(BF16), wout_ref[...])
    up = _dot(_rms(h, gmlp_ref[...]).astype(BF16), wup_ref[...])
    h = h + _dot(jnp.square(jnp.maximum(up, 0.0)).astype(BF16), wdown_ref[...])
    gate = _sigmoid(_dot(_rms(h, gple_ref[...]).astype(BF16), wpg_ref[...]))
    h = h + gate * _dot(p_ref[...].astype(BF16), wpp_ref[...])
    if final:
        h = _rms(h, gfin_ref[...])
    out_ref[...] = h


def _post(h, o, y, ga, gb, p, layer, w, g_final, final):
    t, d = h.shape
    tm = min(POST_TOKEN_BLOCK, t)
    assert t % tm == 0
    tok = lambda wd: pl.BlockSpec((tm, wd), lambda i: (i, 0))
    names = ('w_ret_o', 'w_glu', 'w_out', 'g_mlp', 'w_up', 'w_down', 'g_ple', 'w_ple_gate', 'w_ple_proj')
    return pl.pallas_call(
        functools.partial(_post_body, final=final),
        grid=(t // tm,),
        in_specs=[tok(d), tok(V_W), pl.BlockSpec((y.shape[0], tm, LANES), lambda i: (0, i, 0)), tok(d), tok(d),
                  pl.BlockSpec((None, tm, p.shape[2]), lambda i: (layer, i, 0))]
                 + [_layer_resident(w[n].shape, layer) for n in names] + [_resident(g_final.shape)],
        out_specs=tok(d),
        out_shape=jax.ShapeDtypeStruct((t, d), F32),
        compiler_params=pltpu.CompilerParams(dimension_semantics=("arbitrary",), vmem_limit_bytes=VMEM_LIMIT),
        name="post",
    )(h, o, y, ga, gb, p, *[w[n] for n in names], g_final)


def _rope_tables(pos, n_rows):
    half = RET_DK // 2
    inv = ROPE_BASE ** (-jnp.arange(half, dtype=F32) / half)
    ang = pos.astype(F32)[:, None] * inv[None, :]
    cos = jnp.cos(ang)
    sin = jnp.sin(ang)
    reps = max(1, n_rows // pos.shape[0])
    return (jnp.tile(jnp.concatenate([cos, cos], axis=1), (reps, 1)),
            jnp.tile(jnp.concatenate([-sin, sin], axis=1), (reps, 1)))


def _run_trunk(x, p, pos, s_ret0, s5_re0, s5_im0, w, s5_ops, g_final):
    b, l, d = x.shape
    t = b * l
    depth = w['w_in'].shape[0]
    cos_t, sin_t = _rope_tables(pos, min(TOKEN_BLOCK, t))
    seg_is_row = l == S5_CHUNK
    assert seg_is_row or l % (8 * S5_CHUNK) == 0
    if s5_re0 is None:
        x0_all = jnp.zeros((depth, b, S5_GROUPS, 2 * S5_STATE), F32)
    else:
        x0_all = jnp.concatenate([s5_re0, s5_im0], axis=-1)
    p = p.reshape(depth, t, -1)
    h = x.reshape(t, d)
    rets, s5s = [], []
    for i in range(depth):
        q, k, v, g, u, ga, gb = _in_proj(h, i, w, cos_t, sin_t, l)
        o, s_ret = _retention(q.reshape(b, l, Q_W), k.reshape(b, l, Q_W), v.reshape(b, l, V_W),
                              g.reshape(b, l, V_W), s_ret0, i, w)
        if seg_is_row:
            y, xf = _s5(u, x0_all[i].transpose(1, 0, 2)[None], s5_ops, i, True)
            xf = xf[0].transpose(1, 0, 2)
        else:
            y, xf = _s5(u, x0_all[i][:, :, None, :], s5_ops, i, False)
            xf = xf[:, :, 0, :]
        h = _post(h, o.reshape(t, V_W), y, ga, gb, p, i, w, g_final, final=(i == depth - 1))
        rets.append(s_ret)
        s5s.append(xf)
    s5s = jnp.stack(s5s)
    return h.reshape(b, l, d), jnp.stack(rets), s5s[..., :S5_STATE], s5s[..., S5_STATE:]


def _prepare_weights(g_mix, w_in, g_ret_head, w_ret_o, w_glu_a, w_glu_b, w_out, g_mlp, w_up, w_down, g_ple,
                     w_ple_gate, w_ple_proj):
    row = lambda g: g.reshape(g.shape[0], 1, g.shape[1])
    return {
        'g_mix': row(g_mix), 'w_in': w_in.astype(BF16), 'g_ret_head': row(g_ret_head),
        'w_ret_o': w_ret_o.astype(BF16),
        'w_glu': jnp.concatenate([w_glu_a.astype(BF16), w_glu_b.astype(BF16)], axis=2),
        'w_out': w_out.astype(BF16), 'g_mlp': row(g_mlp), 'w_up': w_up.astype(BF16),
        'w_down': w_down.astype(BF16), 'g_ple': row(g_ple), 'w_ple_gate': w_ple_gate.astype(BF16),
        'w_ple_proj': w_ple_proj.astype(BF16),
    }


def kernel(x_prompt, x_sample, state_ret, state_s5_re, state_s5_im, p_prompt, p_sample, g_mix, w_in, g_ret_head, w_ret_o, s5_a_re, s5_a_im, s5_log_dt, s5_b_re, s5_b_im, s5_c_re, s5_c_im, s5_d, w_glu_a, w_glu_b, w_out, g_mlp, w_up, w_down, g_ple, w_ple_gate, w_ple_proj, g_final):
    d = x_prompt.shape[-1]
    w = _prepare_weights(g_mix, w_in, g_ret_head, w_ret_o, w_glu_a, w_glu_b, w_out, g_mlp, w_up, w_down, g_ple,
                         w_ple_gate, w_ple_proj)
    max_rows = max(x_prompt.shape[1], x_sample.shape[1]) // S5_CHUNK
    s5_ops = _s5_operators(s5_a_re, s5_a_im, s5_log_dt, s5_b_re, s5_b_im, s5_c_re, s5_c_im, s5_d,
                           (max_rows - 1).bit_length())
    g_fin = g_final.reshape(1, d)
    pos_prompt = jnp.arange(x_prompt.shape[1])
    pos_sample = PAST_LEN + jnp.arange(x_sample.shape[1])
    y_p, ret_p, s5re_p, s5im_p = _run_trunk(x_prompt, p_prompt, pos_prompt, None, None, None, w, s5_ops, g_fin)
    y_s, ret_s, s5re_s, s5im_s = _run_trunk(x_sample, p_sample, pos_sample, state_ret, state_s5_re, state_s5_im,
                                            w, s5_ops, g_fin)
    return (y_p, y_s, ret_p, s5re_p, s5im_p, ret_s, s5re_s, s5im_s)
```

```python
import functools
import math

import jax
import jax.numpy as jnp
import numpy as np
from jax import lax
from jax.experimental import pallas as pl
from jax.experimental.pallas import tpu as pltpu

F32 = jnp.float32
BF16 = jnp.bfloat16
HIGHEST = lax.Precision.HIGHEST

EPS = 1e-6
ROPE_BASE = 10000.0
PAST_LEN = 2048
RET_HEADS = 4
RET_DK = 128
RET_DV = 256
Q_W = RET_HEADS * RET_DK
V_W = RET_HEADS * RET_DV
S5_GROUP = 16
S5_GROUPS = 32
S5_WIDTH = S5_GROUP * S5_GROUPS
S5_STATE = 64
S5_CHUNK = 16
S5_COLS = S5_CHUNK * S5_GROUP
LANES = 128
SLOTS = LANES // S5_GROUP
S5_LANE_CHUNKS = S5_WIDTH // LANES
S5_FOLD_ROWS = 32
S5_GROUP_UNROLL = 4
VMEM_LIMIT = 56 * 1024 * 1024

RET_CHUNK = 256
RET_CHUNKS_PER_STEP = 2
RET_SHORT_SEQS = 8
TOKEN_BLOCK = 512
POST_TOKEN_BLOCK = 256


def _resident(shape):
    nd = len(shape)
    return pl.BlockSpec(shape, lambda *_: (0,) * nd, pipeline_mode=pl.Buffered(1))


def _layer_resident(shape, layer):
    nd = len(shape)
    return pl.BlockSpec((None,) + tuple(shape[1:]), lambda *_: (layer,) + (0,) * (nd - 1),
                        pipeline_mode=pl.Buffered(1))


def _sigmoid(x):
    return 1.0 / (1.0 + jnp.exp(-x))


def _rms(x, g):
    return (x * lax.rsqrt(jnp.mean(x * x, axis=-1, keepdims=True) + EPS)) * g


def _dot(a, b):
    return jnp.dot(a, b, preferred_element_type=F32)


def _in_proj_body(x_ref, gm_ref, w_ref, cos_ref, sin_ref, q_ref, k_ref, v_ref, g_ref, u_ref, ga_ref, gb_ref):
    hn = _rms(x_ref[...], gm_ref[...]).astype(BF16)
    cos = cos_ref[...]
    sin = sin_ref[...]

    def proj(lo, hi):
        return _dot(hn, w_ref[:, lo:hi])

    def rope(z, hd):
        xh = z[:, hd * RET_DK:(hd + 1) * RET_DK]
        return xh * cos + pltpu.roll(xh, RET_DK // 2, 1) * sin

    zq = proj(0, Q_W)
    zk = proj(Q_W, 2 * Q_W)
    for hd in range(RET_HEADS):
        sl = slice(hd * RET_DK, (hd + 1) * RET_DK)
        q_ref[:, sl] = rope(zq, hd).astype(BF16)
        k_ref[:, sl] = (rope(zk, hd) * (RET_DK ** -0.5)).astype(BF16)
    o = 2 * Q_W
    v_ref[...] = proj(o, o + V_W).astype(BF16)
    g_ref[...] = proj(o + V_W, o + 2 * V_W).astype(BF16)
    o += 2 * V_W
    zu = proj(o, o + S5_WIDTH)
    for vc in range(S5_LANE_CHUNKS):
        u_ref[vc] = zu[:, vc * LANES:(vc + 1) * LANES]
    o += S5_WIDTH
    d = x_ref.shape[1]
    ga_ref[...] = proj(o, o + d).astype(BF16)
    gb_ref[...] = proj(o + d, o + 2 * d).astype(BF16)


def _in_proj(h, layer, w, cos_t, sin_t, seq_len):
    t, d = h.shape
    tm = min(TOKEN_BLOCK, t)
    assert t % tm == 0 and (seq_len % tm == 0 or tm % seq_len == 0)
    n_pos_blocks = cos_t.shape[0] // tm
    tok = lambda wd: pl.BlockSpec((tm, wd), lambda i: (i, 0))
    pos = pl.BlockSpec((tm, RET_DK), lambda i: (i % n_pos_blocks, 0))
    bf = lambda wd: (tok(wd), jax.ShapeDtypeStruct((t, wd), BF16))
    u_out = (pl.BlockSpec((S5_LANE_CHUNKS, tm, LANES), lambda i: (0, i, 0)),
             jax.ShapeDtypeStruct((S5_LANE_CHUNKS, t, LANES), F32))
    outs = (bf(Q_W), bf(Q_W), bf(V_W), bf(V_W), u_out, bf(d), bf(d))
    return pl.pallas_call(
        _in_proj_body,
        grid=(t // tm,),
        in_specs=[tok(d), _layer_resident(w['g_mix'].shape, layer), _layer_resident(w['w_in'].shape, layer),
                  pos, pos],
        out_specs=[o[0] for o in outs],
        out_shape=[o[1] for o in outs],
        compiler_params=pltpu.CompilerParams(dimension_semantics=("arbitrary",), vmem_limit_bytes=VMEM_LIMIT),
        name="in_proj",
    )(h, w['g_mix'], w['w_in'], cos_t, sin_t)


def _retention_body(*refs, chunk, n_chunks, n_seq, zero_init):
    if zero_init:
        q_ref, k_ref, v_ref, o_ref, s_ref, inner_scr, qd_scr, kd_scr = refs
    else:
        q_ref, k_ref, v_ref, s0_ref, o_ref, s_ref, inner_scr, qd_scr, kd_scr = refs
    c = chunk

    @pl.when(pl.program_id(1) == 0)
    def _():
        s_ref[...] = jnp.zeros(s_ref.shape, F32) if zero_init else s0_ref[...]
        row_cc = lax.broadcasted_iota(jnp.int32, (c, c), 0)
        col_cc = lax.broadcasted_iota(jnp.int32, (c, c), 1)
        diff = (row_cc - col_cc).astype(F32)
        idx_v = lax.broadcasted_iota(jnp.int32, (c, RET_DV), 0).astype(F32)
        idx_k = lax.broadcasted_iota(jnp.int32, (c, RET_DK), 0).astype(F32)
        for hd in range(RET_HEADS):
            lg = math.log1p(-(2.0 ** (-5.0 - hd)))
            inner_scr[hd] = jnp.where(diff >= 0.0, jnp.exp(jnp.maximum(diff, 0.0) * lg), 0.0)
            qd_scr[hd] = jnp.exp((idx_v + 1.0) * lg)
            kd_scr[hd] = jnp.exp((c - 1.0 - idx_k) * lg)

    for sq in range(n_seq):
        for ci in range(n_chunks):
            rows = slice(ci * c, (ci + 1) * c)
            for hd in range(RET_HEADS):
                lg = math.log1p(-(2.0 ** (-5.0 - hd)))
                q = q_ref[sq, rows, hd * RET_DK:(hd + 1) * RET_DK]
                k = k_ref[sq, rows, hd * RET_DK:(hd + 1) * RET_DK]
                v = v_ref[sq, rows, hd * RET_DV:(hd + 1) * RET_DV]
                s = s_ref[sq, hd]
                scores = lax.dot_general(q, k, (((1,), (1,)), ((), ())), preferred_element_type=F32)
                o = _dot((scores * inner_scr[hd]).astype(BF16), v) + qd_scr[hd] * _dot(q, s.astype(BF16))
                kd = (k.astype(F32) * kd_scr[hd]).astype(BF16)
                s_ref[sq, hd] = math.exp(c * lg) * s + lax.dot_general(
                    kd, v, (((0,), (0,)), ((), ())), preferred_element_type=F32)
                o_ref[sq, rows, hd * RET_DV:(hd + 1) * RET_DV] = o.astype(BF16)


def _retention(q, k, v, s0, layer):
    b, l, _ = q.shape
    c = min(RET_CHUNK, l)
    n_chunks = min(RET_CHUNKS_PER_STEP, l // c)
    tl = c * n_chunks
    assert l % tl == 0
    n_seq = RET_SHORT_SEQS if (l == c and b % RET_SHORT_SEQS == 0) else 1
    seq = lambda wd: pl.BlockSpec((n_seq, tl, wd), lambda i, j: (i, j, 0))
    st = pl.BlockSpec((n_seq, RET_HEADS, RET_DK, RET_DV), lambda i, j: (i, 0, 0, 0))
    st_in = pl.BlockSpec((None, n_seq, RET_HEADS, RET_DK, RET_DV), lambda i, j: (layer, i, 0, 0, 0))
    zero_init = s0 is None
    return pl.pallas_call(
        functools.partial(_retention_body, chunk=c, n_chunks=n_chunks, n_seq=n_seq, zero_init=zero_init),
        grid=(b // n_seq, l // tl),
        in_specs=[seq(Q_W), seq(Q_W), seq(V_W)] + ([] if zero_init else [st_in]),
        out_specs=[seq(V_W), st],
        out_shape=[jax.ShapeDtypeStruct((b, l, V_W), BF16),
                   jax.ShapeDtypeStruct((b, RET_HEADS, RET_DK, RET_DV), F32)],
        scratch_shapes=[pltpu.VMEM((RET_HEADS, c, c), F32), pltpu.VMEM((RET_HEADS, c, RET_DV), F32),
                        pltpu.VMEM((RET_HEADS, c, RET_DK), F32)],
        compiler_params=pltpu.CompilerParams(dimension_semantics=("arbitrary", "arbitrary"),
                                             vmem_limit_bytes=VMEM_LIMIT),
        name="retention",
    )(q, k, v, *([] if zero_init else [s0]))


def _rotate_slots(x, axis):
    n_l = x.shape[0]
    shp = x.shape
    split = shp[2:axis] + (shp[axis] // LANES, LANES) + shp[axis + 1:]
    x = x.reshape((n_l, S5_GROUPS // SLOTS, SLOTS) + split)
    parts = [jnp.roll(x[:, :, res], S5_GROUP * res, axis=axis + 1) for res in range(SLOTS)]
    return jnp.stack(parts, axis=2).reshape(shp)


def _s5_operators(a_re, a_im, log_dt, b_re, b_im, c_re, c_im, d_skip, levels):
    n_pos = S5_CHUNK
    n_l = a_re.shape[0]
    dt = jnp.exp(log_dt)[..., None]
    mag = jnp.exp(a_re * dt)
    lr = mag * jnp.cos(a_im * dt)
    li = mag * jnp.sin(a_im * dt)
    den = a_re * a_re + a_im * a_im
    fr = ((lr - 1.0) * a_re + li * a_im) / den
    fi = (li * a_re - (lr - 1.0) * a_im) / den
    bb_re = fr[..., None] * b_re - fi[..., None] * b_im
    bb_im = fr[..., None] * b_im + fi[..., None] * b_re
    bt_re = bb_re.transpose(0, 1, 3, 2)
    bt_im = bb_im.transpose(0, 1, 3, 2)

    def lam_pow(kk):
        kk = jnp.asarray(kk, F32)[None, None, :, None]
        m = jnp.exp(kk * (a_re * dt)[:, :, None])
        return m * jnp.cos(kk * (a_im * dt)[:, :, None]), m * jnp.sin(kk * (a_im * dt)[:, :, None])

    pr, pi = lam_pow(np.arange(n_pos + 1))
    cp_re = c_re[:, :, None] * pr[:, :, :n_pos, None] - c_im[:, :, None] * pi[:, :, :n_pos, None]
    cp_im = c_re[:, :, None] * pi[:, :, :n_pos, None] + c_im[:, :, None] * pr[:, :, :n_pos, None]
    cp_cat = jnp.concatenate([cp_re, -cp_im], axis=-1).reshape(n_l, S5_GROUPS, S5_COLS, 2 * S5_STATE)
    bt_cat = jnp.concatenate([bt_re, bt_im], axis=-1)
    kcat = jnp.einsum('lgcn,lgxn->lgcx', bt_cat, cp_cat, precision=HIGHEST)
    kpad = jnp.pad(kcat, ((0, 0), (0, 0), (0, 0), (S5_COLS, 0)))
    m_nat = jnp.concatenate(
        [kpad[..., S5_COLS - S5_GROUP * s:2 * S5_COLS - S5_GROUP * s] for s in range(n_pos)], axis=2)
    rr = pr[:, :, n_pos - 1::-1][:, :, :, None]
    ri = pi[:, :, n_pos - 1::-1][:, :, :, None]
    g_re = rr * bt_re[:, :, None] - ri * bt_im[:, :, None]
    g_im = rr * bt_im[:, :, None] + ri * bt_re[:, :, None]
    g_nat = jnp.concatenate([g_re, g_im], axis=-1).reshape(n_l, S5_GROUPS, S5_COLS, 2 * S5_STATE)
    pt_re = pr[:, :, 1:].transpose(0, 1, 3, 2)[..., None]
    pt_im = pi[:, :, 1:].transpose(0, 1, 3, 2)[..., None]
    ct_re = c_re.transpose(0, 1, 3, 2)[:, :, :, None]
    ct_im = c_im.transpose(0, 1, 3, 2)[:, :, :, None]
    hp_re = ct_re * pt_re - ct_im * pt_im
    hp_im = ct_re * pt_im + ct_im * pt_re
    h_nat = jnp.concatenate([hp_re, -hp_im], axis=2).reshape(n_l, S5_GROUPS, 2 * S5_STATE, S5_COLS)
    m_op = _rotate_slots(_rotate_slots(m_nat, 2), 3)
    g_op = _rotate_slots(g_nat, 2)
    h_op = _rotate_slots(h_nat, 3)
    dr, di = lam_pow(n_pos * (2 ** np.arange(max(levels, 1))))
    d_a = jnp.concatenate([dr, dr], axis=-1)
    d_b = jnp.concatenate([-di, di], axis=-1)
    d_tile = jnp.tile(d_skip.reshape(n_l, S5_GROUPS, 1, S5_GROUP), (1, 1, S5_CHUNK, 1)).reshape(
        n_l, S5_GROUPS, 1, S5_COLS)
    return m_op.astype(BF16), g_op.astype(BF16), h_op.astype(BF16), d_a, d_b, d_tile


def _s5_body(u_ref, x0_ref, m_ref, g_ref, h_ref, da_ref, db_ref, dsk_ref, y_ref, xf_ref, v_scr, y_scr,
             *, rows, seg, levels):
    r = rows
    half = S5_STATE
    rp = min(S5_FOLD_ROWS, r)
    slot = lax.broadcasted_iota(jnp.int32, (rp, LANES), 1) // S5_GROUP
    masks = [slot == s for s in range(SLOTS)]
    n_col = S5_CHUNK // SLOTS

    for vc in range(S5_LANE_CHUNKS):
        for col in range(n_col):
            for r0 in range(0, r, rp):
                srcs = []
                for tt in range(SLOTS):
                    a = u_ref[vc, pl.ds(col * SLOTS + tt + S5_CHUNK * r0, rp, stride=S5_CHUNK), :]
                    srcs.append(pltpu.roll(a, S5_GROUP * tt, 1) if tt else a)
                for kk in range(SLOTS):
                    acc = srcs[0]
                    for tt in range(1, SLOTS):
                        acc = jnp.where(masks[(kk + tt) % SLOTS], srcs[tt], acc)
                    v_scr[vc * SLOTS + kk, r0:r0 + rp, col * LANES:(col + 1) * LANES] = acc

    row = lax.broadcasted_iota(jnp.int32, (r, 2 * half), 0)

    def cmul(x, a, b):
        return x * a + pltpu.roll(x, half, 1) * b

    def group_step(g, carry):
        vf = v_scr[g]
        vb = vf.astype(BF16)
        y_intra = _dot(vb, m_ref[g])
        w = _dot(vb, g_ref[g])
        x0 = x0_ref[0, g]
        da = da_ref[g]
        db = db_ref[g]
        inj = cmul(x0, da[0:1], db[0:1])
        if seg == 1:
            w = w + inj
            x_in = x0
        else:
            w = w + jnp.where(row == 0, inj, 0.0)
            ws = pltpu.roll(w, half, 1)
            for kk in range(levels):
                s = 1 << kk
                a = da[kk:kk + 1]
                b = db[kk:kk + 1]
                sh = jnp.where(row >= s, pltpu.roll(w, s, 0), 0.0)
                shs = jnp.where(row >= s, pltpu.roll(ws, s, 0), 0.0)
                w, ws = w + (sh * a + shs * b), ws + (shs * a - sh * b)
            x_in = jnp.where(row == 0, x0, pltpu.roll(w, 1, 0))
        xf_ref[0, g] = w if seg == 1 else w[r - 1:r, :]
        y_scr[g] = y_intra + _dot(x_in.astype(BF16), h_ref[g]) + vf * dsk_ref[g]
        return carry

    lax.fori_loop(0, S5_GROUPS, group_step, 0, unroll=S5_GROUP_UNROLL)

    for vc in range(S5_LANE_CHUNKS):
        for col in range(n_col):
            for r0 in range(0, r, rp):
                ys = [y_scr[vc * SLOTS + kk, r0:r0 + rp, col * LANES:(col + 1) * LANES] for kk in range(SLOTS)]
                for tt in range(SLOTS):
                    acc = ys[(-tt) % SLOTS]
                    for sg in range(1, SLOTS):
                        acc = jnp.where(masks[sg], ys[(sg - tt) % SLOTS], acc)
                    if tt:
                        acc = pltpu.roll(acc, LANES - S5_GROUP * tt, 1)
                    y_ref[vc, pl.ds(col * SLOTS + tt + S5_CHUNK * r0, rp, stride=S5_CHUNK), :] = acc


def _s5(u, x0, ops, layer, seg_is_row):
    nb = x0.shape[0]
    n_tok = u.shape[1] // nb
    rows = n_tok // S5_CHUNK
    seg = 1 if seg_is_row else rows
    nseq = rows // seg
    levels = 0 if seg == 1 else (rows - 1).bit_length()
    assert rows % min(S5_FOLD_ROWS, rows) == 0 and ops[3].shape[2] >= max(levels, 1)
    tok = pl.BlockSpec((S5_LANE_CHUNKS, n_tok, LANES), lambda i: (0, i, 0))
    st = pl.BlockSpec((1, S5_GROUPS, nseq, 2 * S5_STATE), lambda i: (i, 0, 0, 0))
    return pl.pallas_call(
        functools.partial(_s5_body, rows=rows, seg=seg, levels=levels),
        grid=(nb,),
        in_specs=[tok, st] + [_layer_resident(o.shape, layer) for o in ops],
        out_specs=[tok, st],
        out_shape=[jax.ShapeDtypeStruct(u.shape, F32),
                   jax.ShapeDtypeStruct((nb, S5_GROUPS, nseq, 2 * S5_STATE), F32)],
        scratch_shapes=[pltpu.VMEM((S5_GROUPS, rows, S5_COLS), F32),
                        pltpu.VMEM((S5_GROUPS, rows, S5_COLS), F32)],
        compiler_params=pltpu.CompilerParams(dimension_semantics=("arbitrary",), vmem_limit_bytes=VMEM_LIMIT),
        name="s5",
    )(u, x0, *ops)


def _gelu_tanh(x):
    return 0.5 * x * (1.0 + jnp.tanh(math.sqrt(2.0 / math.pi) * (x + 0.044715 * (x * x * x))))


def _post_body(h_ref, o_ref, g_ref, y_ref, ga_ref, gb_ref, p_ref, gh_ref, wro_ref, wglu_ref, wout_ref, gmlp_ref,
               wup_ref, wdown_ref, gple_ref, wpg_ref, wpp_ref, gfin_ref, out_ref, *, final):
    d = h_ref.shape[1]
    h = h_ref[...]
    gated = []
    for hd in range(RET_HEADS):
        sl = slice(hd * RET_DV, (hd + 1) * RET_DV)
        gate = g_ref[:, sl].astype(F32)
        gated.append((gate * _sigmoid(gate) * _rms(o_ref[:, sl].astype(F32), gh_ref[:, sl])).astype(BF16))
    branch_a = _dot(jnp.concatenate(gated, axis=1), wro_ref[...])
    y = jnp.concatenate([y_ref[vc] for vc in range(y_ref.shape[0])], axis=1)
    glu = _dot(_gelu_tanh(y).astype(BF16), wglu_ref[...])
    branch_b = glu[:, :d] * _sigmoid(glu[:, d:])
    merged = (_sigmoid(ga_ref[...].astype(F32)) * branch_a + _sigmoid(gb_ref[...].astype(F32)) * branch_b)
    h = h + _dot(merged.astype(BF16), wout_ref[...])
    up = _dot(_rms(h, gmlp_ref[...]).astype(BF16), wup_ref[...])
    h = h + _dot(jnp.square(jnp.maximum(up, 0.0)).astype(BF16), wdown_ref[...])
    gate = _sigmoid(_dot(_rms(h, gple_ref[...]).astype(BF16), wpg_ref[...]))
    h = h + gate * _dot(p_ref[...].astype(BF16), wpp_ref[...])
    if final:
        h = _rms(h, gfin_ref[...])
    out_ref[...] = h


def _post(h, o, g, y, ga, gb, p, layer, w, g_final, final):
    t, d = h.shape
    tm = min(POST_TOKEN_BLOCK, t)
    assert t % tm == 0
    tok = lambda wd: pl.BlockSpec((tm, wd), lambda i: (i, 0))
    names = ('g_ret_head', 'w_ret_o', 'w_glu', 'w_out', 'g_mlp', 'w_up', 'w_down', 'g_ple', 'w_ple_gate',
             'w_ple_proj')
    return pl.pallas_call(
        functools.partial(_post_body, final=final),
        grid=(t // tm,),
        in_specs=[tok(d), tok(V_W), tok(V_W), pl.BlockSpec((y.shape[0], tm, LANES), lambda i: (0, i, 0)), tok(d),
                  tok(d), pl.BlockSpec((None, tm, p.shape[2]), lambda i: (layer, i, 0))]
                 + [_layer_resident(w[n].shape, layer) for n in names] + [_resident(g_final.shape)],
        out_specs=tok(d),
        out_shape=jax.ShapeDtypeStruct((t, d), F32),
        compiler_params=pltpu.CompilerParams(dimension_semantics=("arbitrary",), vmem_limit_bytes=VMEM_LIMIT),
        name="post",
    )(h, o, g, y, ga, gb, p, *[w[n] for n in names], g_final)


def _rope_tables(pos, n_rows):
    half = RET_DK // 2
    inv = ROPE_BASE ** (-jnp.arange(half, dtype=F32) / half)
    ang = pos.astype(F32)[:, None] * inv[None, :]
    cos = jnp.cos(ang)
    sin = jnp.sin(ang)
    reps = max(1, n_rows // pos.shape[0])
    return (jnp.tile(jnp.concatenate([cos, cos], axis=1), (reps, 1)),
            jnp.tile(jnp.concatenate([-sin, sin], axis=1), (reps, 1)))


def _run_trunk(x, p, pos, s_ret0, s5_re0, s5_im0, w, s5_ops, g_final):
    b, l, d = x.shape
    t = b * l
    depth = w['w_in'].shape[0]
    cos_t, sin_t = _rope_tables(pos, min(TOKEN_BLOCK, t))
    seg_is_row = l == S5_CHUNK
    assert seg_is_row or l % (8 * S5_CHUNK) == 0
    if s5_re0 is None:
        x0_all = jnp.zeros((depth, b, S5_GROUPS, 2 * S5_STATE), F32)
    else:
        x0_all = jnp.concatenate([s5_re0, s5_im0], axis=-1)
    p = p.reshape(depth, t, -1)
    h = x.reshape(t, d)
    rets, s5s = [], []
    for i in range(depth):
        q, k, v, g, u, ga, gb = _in_proj(h, i, w, cos_t, sin_t, l)
        o, s_ret = _retention(q.reshape(b, l, Q_W), k.reshape(b, l, Q_W), v.reshape(b, l, V_W), s_ret0, i)
        if seg_is_row:
            y, xf = _s5(u, x0_all[i].transpose(1, 0, 2)[None], s5_ops, i, True)
            xf = xf[0].transpose(1, 0, 2)
        else:
            y, xf = _s5(u, x0_all[i][:, :, None, :], s5_ops, i, False)
            xf = xf[:, :, 0, :]
        h = _post(h, o.reshape(t, V_W), g, y, ga, gb, p, i, w, g_final, final=(i == depth - 1))
        rets.append(s_ret)
        s5s.append(xf)
    s5s = jnp.stack(s5s)
    return h.reshape(b, l, d), jnp.stack(rets), s5s[..., :S5_STATE], s5s[..., S5_STATE:]


def _prepare_weights(g_mix, w_in, g_ret_head, w_ret_o, w_glu_a, w_glu_b, w_out, g_mlp, w_up, w_down, g_ple,
                     w_ple_gate, w_ple_proj):
    row = lambda g: g.reshape(g.shape[0], 1, g.shape[1])
    return {
        'g_mix': row(g_mix), 'w_in': w_in.astype(BF16), 'g_ret_head': row(g_ret_head),
        'w_ret_o': w_ret_o.astype(BF16),
        'w_glu': jnp.concatenate([w_glu_a.astype(BF16), w_glu_b.astype(BF16)], axis=2),
        'w_out': w_out.astype(BF16), 'g_mlp': row(g_mlp), 'w_up': w_up.astype(BF16),
        'w_down': w_down.astype(BF16), 'g_ple': row(g_ple), 'w_ple_gate': w_ple_gate.astype(BF16),
        'w_ple_proj': w_ple_proj.astype(BF16),
    }


def kernel(x_prompt, x_sample, state_ret, state_s5_re, state_s5_im, p_prompt, p_sample, g_mix, w_in, g_ret_head, w_ret_o, s5_a_re, s5_a_im, s5_log_dt, s5_b_re, s5_b_im, s5_c_re, s5_c_im, s5_d, w_glu_a, w_glu_b, w_out, g_mlp, w_up, w_down, g_ple, w_ple_gate, w_ple_proj, g_final):
    d = x_prompt.shape[-1]
    w = _prepare_weights(g_mix, w_in, g_ret_head, w_ret_o, w_glu_a, w_glu_b, w_out, g_mlp, w_up, w_down, g_ple,
                         w_ple_gate, w_ple_proj)
    max_rows = max(x_prompt.shape[1], x_sample.shape[1]) // S5_CHUNK
    s5_ops = _s5_operators(s5_a_re, s5_a_im, s5_log_dt, s5_b_re, s5_b_im, s5_c_re, s5_c_im, s5_d,
                           (max_rows - 1).bit_length())
    g_fin = g_final.reshape(1, d)
    pos_prompt = jnp.arange(x_prompt.shape[1])
    pos_sample = PAST_LEN + jnp.arange(x_sample.shape[1])
    y_p, ret_p, s5re_p, s5im_p = _run_trunk(x_prompt, p_prompt, pos_prompt, None, None, None, w, s5_ops, g_fin)
    y_s, ret_s, s5re_s, s5im_s = _run_trunk(x_sample, p_sample, pos_sample, state_ret, state_s5_re, state_s5_im,
                                            w, s5_ops, g_fin)
    return (y_p, y_s, ret_p, s5re_p, s5im_p, ret_s, s5re_s, s5im_s)
```

```python
import functools
import math

import jax
import jax.numpy as jnp
import numpy as np
from jax import lax
from jax.experimental import pallas as pl
from jax.experimental.pallas import tpu as pltpu

F32 = jnp.float32
BF16 = jnp.bfloat16
HIGHEST = lax.Precision.HIGHEST

EPS = 1e-6
ROPE_BASE = 10000.0
PAST_LEN = 2048
RET_HEADS = 4
RET_DK = 128
RET_DV = 256
Q_W = RET_HEADS * RET_DK
V_W = RET_HEADS * RET_DV
S5_GROUP = 16
S5_GROUPS = 32
S5_WIDTH = S5_GROUP * S5_GROUPS
S5_STATE = 64
S5_CHUNK = 16
S5_COLS = S5_CHUNK * S5_GROUP
LANES = 128
SUBLANES = 8
SLOTS = LANES // S5_GROUP
S5_LANE_CHUNKS = S5_WIDTH // LANES
S5_FOLD_ROWS = 32
S5_GROUP_UNROLL = 4
VMEM_LIMIT = 56 * 1024 * 1024

RET_CHUNK = 256
RET_CHUNKS_PER_STEP = 2
RET_SHORT_SEQS = 8
TOKEN_BLOCK = 512
POST_TOKEN_BLOCK = 256


def _resident(shape):
    nd = len(shape)
    return pl.BlockSpec(shape, lambda *_: (0,) * nd, pipeline_mode=pl.Buffered(1))


def _layer_resident(shape, layer):
    nd = len(shape)
    return pl.BlockSpec((None,) + tuple(shape[1:]), lambda *_: (layer,) + (0,) * (nd - 1),
                        pipeline_mode=pl.Buffered(1))


def _sigmoid(x):
    return 1.0 / (1.0 + jnp.exp(-x))


def _inv_rms(x):
    return lax.rsqrt(jnp.mean(x * x, axis=-1, keepdims=True) + EPS)


def _rms(x, g):
    return (x * _inv_rms(x)) * g


def _dot(a, b):
    return jnp.dot(a, b, preferred_element_type=F32)


def _in_proj_body(x_ref, gm_ref, w_ref, cos_ref, sin_ref, q_ref, k_ref, v_ref, g_ref, u_ref, ga_ref, gb_ref):
    x = x_ref[...]
    inv = _inv_rms(x)
    xg = (x * gm_ref[...]).astype(BF16)
    cos = cos_ref[...]
    sin = sin_ref[...]
    d = x_ref.shape[1]

    def proj(lo, hi):
        return inv * _dot(xg, w_ref[:, lo:hi])

    def rope(z, hd):
        xh = z[:, hd * RET_DK:(hd + 1) * RET_DK]
        return xh * cos + pltpu.roll(xh, RET_DK // 2, 1) * sin

    o_v = 2 * Q_W
    o_u = o_v + 2 * V_W
    o_gate = o_u + S5_WIDTH
    ga_ref[...] = _sigmoid(proj(o_gate, o_gate + d)).astype(BF16)
    gb_ref[...] = _sigmoid(proj(o_gate + d, o_gate + 2 * d)).astype(BF16)
    zg = proj(o_v + V_W, o_v + 2 * V_W)
    g_ref[...] = (zg * _sigmoid(zg)).astype(BF16)
    zq = proj(0, Q_W)
    zk = proj(Q_W, 2 * Q_W)
    for hd in range(RET_HEADS):
        sl = slice(hd * RET_DK, (hd + 1) * RET_DK)
        q_ref[:, sl] = rope(zq, hd).astype(BF16)
        k_ref[:, sl] = (rope(zk, hd) * (RET_DK ** -0.5)).astype(BF16)
    v_ref[...] = proj(o_v, o_v + V_W).astype(BF16)
    zu = proj(o_u, o_u + S5_WIDTH)
    for vc in range(S5_LANE_CHUNKS):
        u_ref[vc] = zu[:, vc * LANES:(vc + 1) * LANES]


def _in_proj(h, layer, w, cos_t, sin_t, seq_len):
    t, d = h.shape
    tm = min(TOKEN_BLOCK, t)
    assert t % tm == 0 and (seq_len % tm == 0 or tm % seq_len == 0)
    n_pos_blocks = cos_t.shape[0] // tm
    tok = lambda wd: pl.BlockSpec((tm, wd), lambda i: (i, 0))
    pos = pl.BlockSpec((tm, RET_DK), lambda i: (i % n_pos_blocks, 0))
    bf = lambda wd: (tok(wd), jax.ShapeDtypeStruct((t, wd), BF16))
    u_out = (pl.BlockSpec((S5_LANE_CHUNKS, tm, LANES), lambda i: (0, i, 0)),
             jax.ShapeDtypeStruct((S5_LANE_CHUNKS, t, LANES), F32))
    outs = (bf(Q_W), bf(Q_W), bf(V_W), bf(V_W), u_out, bf(d), bf(d))
    return pl.pallas_call(
        _in_proj_body,
        grid=(t // tm,),
        in_specs=[tok(d), _layer_resident(w['g_mix'].shape, layer), _layer_resident(w['w_in'].shape, layer),
                  pos, pos],
        out_specs=[o[0] for o in outs],
        out_shape=[o[1] for o in outs],
        compiler_params=pltpu.CompilerParams(dimension_semantics=("arbitrary",), vmem_limit_bytes=VMEM_LIMIT),
        name="in_proj",
    )(h, w['g_mix'], w['w_in'], cos_t, sin_t)


def _retention_body(*refs, chunk, n_chunks, n_seq, zero_init):
    if zero_init:
        q_ref, k_ref, v_ref, o_ref, s_ref, inner_scr, qd_scr, kd_scr = refs
    else:
        q_ref, k_ref, v_ref, s0_ref, o_ref, s_ref, inner_scr, qd_scr, kd_scr = refs
    c = chunk

    @pl.when(pl.program_id(1) == 0)
    def _():
        s_ref[...] = jnp.zeros(s_ref.shape, F32) if zero_init else s0_ref[...]
        row_cc = lax.broadcasted_iota(jnp.int32, (c, c), 0)
        col_cc = lax.broadcasted_iota(jnp.int32, (c, c), 1)
        diff = (row_cc - col_cc).astype(F32)
        idx_v = lax.broadcasted_iota(jnp.int32, (c, RET_DV), 0).astype(F32)
        idx_k = lax.broadcasted_iota(jnp.int32, (c, RET_DK), 0).astype(F32)
        for hd in range(RET_HEADS):
            lg = math.log1p(-(2.0 ** (-5.0 - hd)))
            inner_scr[hd] = jnp.where(diff >= 0.0, jnp.exp(jnp.maximum(diff, 0.0) * lg), 0.0)
            qd_scr[hd] = jnp.exp((idx_v + 1.0) * lg)
            kd_scr[hd] = jnp.exp((c - 1.0 - idx_k) * lg)

    for sq in range(n_seq):
        for ci in range(n_chunks):
            rows = slice(ci * c, (ci + 1) * c)
            for hd in range(RET_HEADS):
                lg = math.log1p(-(2.0 ** (-5.0 - hd)))
                q = q_ref[sq, rows, hd * RET_DK:(hd + 1) * RET_DK]
                k = k_ref[sq, rows, hd * RET_DK:(hd + 1) * RET_DK]
                v = v_ref[sq, rows, hd * RET_DV:(hd + 1) * RET_DV]
                s = s_ref[sq, hd]
                scores = lax.dot_general(q, k, (((1,), (1,)), ((), ())), preferred_element_type=F32)
                o = _dot((scores * inner_scr[hd]).astype(BF16), v) + qd_scr[hd] * _dot(q, s.astype(BF16))
                kd = (k.astype(F32) * kd_scr[hd]).astype(BF16)
                s_ref[sq, hd] = math.exp(c * lg) * s + lax.dot_general(
                    kd, v, (((0,), (0,)), ((), ())), preferred_element_type=F32)
                o_ref[sq, rows, hd * RET_DV:(hd + 1) * RET_DV] = o.astype(BF16)


def _retention(q, k, v, s0, layer):
    b, l, _ = q.shape
    c = min(RET_CHUNK, l)
    n_chunks = min(RET_CHUNKS_PER_STEP, l // c)
    tl = c * n_chunks
    assert l % tl == 0
    n_seq = RET_SHORT_SEQS if (l == c and b % RET_SHORT_SEQS == 0) else 1
    seq = lambda wd: pl.BlockSpec((n_seq, tl, wd), lambda i, j: (i, j, 0))
    st = pl.BlockSpec((n_seq, RET_HEADS, RET_DK, RET_DV), lambda i, j: (i, 0, 0, 0))
    st_in = pl.BlockSpec((None, n_seq, RET_HEADS, RET_DK, RET_DV), lambda i, j: (layer, i, 0, 0, 0))
    zero_init = s0 is None
    return pl.pallas_call(
        functools.partial(_retention_body, chunk=c, n_chunks=n_chunks, n_seq=n_seq, zero_init=zero_init),
        grid=(b // n_seq, l // tl),
        in_specs=[seq(Q_W), seq(Q_W), seq(V_W)] + ([] if zero_init else [st_in]),
        out_specs=[seq(V_W), st],
        out_shape=[jax.ShapeDtypeStruct((b, l, V_W), BF16),
                   jax.ShapeDtypeStruct((b, RET_HEADS, RET_DK, RET_DV), F32)],
        scratch_shapes=[pltpu.VMEM((RET_HEADS, c, c), F32), pltpu.VMEM((RET_HEADS, c, RET_DV), F32),
                        pltpu.VMEM((RET_HEADS, c, RET_DK), F32)],
        compiler_params=pltpu.CompilerParams(dimension_semantics=("arbitrary", "arbitrary"),
                                             vmem_limit_bytes=VMEM_LIMIT),
        name="retention",
    )(q, k, v, *([] if zero_init else [s0]))


def _rotate_slots(x, axis):
    n_l = x.shape[0]
    shp = x.shape
    split = shp[2:axis] + (shp[axis] // LANES, LANES) + shp[axis + 1:]
    x = x.reshape((n_l, S5_GROUPS // SLOTS, SLOTS) + split)
    parts = [jnp.roll(x[:, :, res], S5_GROUP * res, axis=axis + 1) for res in range(SLOTS)]
    return jnp.stack(parts, axis=2).reshape(shp)


def _s5_operators(a_re, a_im, log_dt, b_re, b_im, c_re, c_im, d_skip, levels):
    n_pos = S5_CHUNK
    n_l = a_re.shape[0]
    dt = jnp.exp(log_dt)[..., None]
    mag = jnp.exp(a_re * dt)
    lr = mag * jnp.cos(a_im * dt)
    li = mag * jnp.sin(a_im * dt)
    den = a_re * a_re + a_im * a_im
    fr = ((lr - 1.0) * a_re + li * a_im) / den
    fi = (li * a_re - (lr - 1.0) * a_im) / den
    bb_re = fr[..., None] * b_re - fi[..., None] * b_im
    bb_im = fr[..., None] * b_im + fi[..., None] * b_re
    bt_re = bb_re.transpose(0, 1, 3, 2)
    bt_im = bb_im.transpose(0, 1, 3, 2)

    def lam_pow(kk):
        kk = jnp.asarray(kk, F32)[None, None, :, None]
        m = jnp.exp(kk * (a_re * dt)[:, :, None])
        return m * jnp.cos(kk * (a_im * dt)[:, :, None]), m * jnp.sin(kk * (a_im * dt)[:, :, None])

    pr, pi = lam_pow(np.arange(n_pos + 1))
    cp_re = c_re[:, :, None] * pr[:, :, :n_pos, None] - c_im[:, :, None] * pi[:, :, :n_pos, None]
    cp_im = c_re[:, :, None] * pi[:, :, :n_pos, None] + c_im[:, :, None] * pr[:, :, :n_pos, None]
    cp_cat = jnp.concatenate([cp_re, -cp_im], axis=-1).reshape(n_l, S5_GROUPS, S5_COLS, 2 * S5_STATE)
    bt_cat = jnp.concatenate([bt_re, bt_im], axis=-1)
    kcat = jnp.einsum('lgcn,lgxn->lgcx', bt_cat, cp_cat, precision=HIGHEST)
    kpad = jnp.pad(kcat, ((0, 0), (0, 0), (0, 0), (S5_COLS, 0)))
    m_nat = jnp.concatenate(
        [kpad[..., S5_COLS - S5_GROUP * s:2 * S5_COLS - S5_GROUP * s] for s in range(n_pos)], axis=2)
    rr = pr[:, :, n_pos - 1::-1][:, :, :, None]
    ri = pi[:, :, n_pos - 1::-1][:, :, :, None]
    g_re = rr * bt_re[:, :, None] - ri * bt_im[:, :, None]
    g_im = rr * bt_im[:, :, None] + ri * bt_re[:, :, None]
    g_nat = jnp.concatenate([g_re, g_im], axis=-1).reshape(n_l, S5_GROUPS, S5_COLS, 2 * S5_STATE)
    pt_re = pr[:, :, 1:].transpose(0, 1, 3, 2)[..., None]
    pt_im = pi[:, :, 1:].transpose(0, 1, 3, 2)[..., None]
    ct_re = c_re.transpose(0, 1, 3, 2)[:, :, :, None]
    ct_im = c_im.transpose(0, 1, 3, 2)[:, :, :, None]
    hp_re = ct_re * pt_re - ct_im * pt_im
    hp_im = ct_re * pt_im + ct_im * pt_re
    h_nat = jnp.concatenate([hp_re, -hp_im], axis=2).reshape(n_l, S5_GROUPS, 2 * S5_STATE, S5_COLS)
    m_op = _rotate_slots(_rotate_slots(m_nat, 2), 3)
    g_op = _rotate_slots(g_nat, 2)
    h_op = _rotate_slots(h_nat, 3)
    dr, di = lam_pow(n_pos * (2 ** np.arange(max(levels, 1))))
    d_a = jnp.concatenate([dr, dr], axis=-1)
    d_b = jnp.concatenate([-di, di], axis=-1)
    d_tile = jnp.tile(d_skip.reshape(n_l, S5_GROUPS, 1, S5_GROUP), (1, 1, S5_CHUNK, 1)).reshape(
        n_l, S5_GROUPS, 1, S5_COLS)
    return m_op.astype(BF16), g_op.astype(BF16), h_op.astype(BF16), d_a, d_b, d_tile


def _s5_body(u_ref, x0_ref, m_ref, g_ref, h_ref, da_ref, db_ref, dsk_ref, y_ref, xf_ref, v_scr, y_scr,
             *, rows, seg, levels):
    r = rows
    half = S5_STATE
    rp = min(S5_FOLD_ROWS, r)
    slot = lax.broadcasted_iota(jnp.int32, (rp, LANES), 1) // S5_GROUP
    masks = [slot == s for s in range(SLOTS)]
    n_col = S5_CHUNK // SLOTS

    for vc in range(S5_LANE_CHUNKS):
        for col in range(n_col):
            for r0 in range(0, r, rp):
                srcs = []
                for tt in range(SLOTS):
                    a = u_ref[vc, pl.ds(col * SLOTS + tt + S5_CHUNK * r0, rp, stride=S5_CHUNK), :]
                    srcs.append(pltpu.roll(a, S5_GROUP * tt, 1) if tt else a)
                for kk in range(SLOTS):
                    acc = srcs[0]
                    for tt in range(1, SLOTS):
                        acc = jnp.where(masks[(kk + tt) % SLOTS], srcs[tt], acc)
                    v_scr[vc * SLOTS + kk, r0:r0 + rp, col * LANES:(col + 1) * LANES] = acc

    row = lax.broadcasted_iota(jnp.int32, (r, 2 * half), 0)

    def cmul(x, a, b):
        return x * a + pltpu.roll(x, half, 1) * b

    def group_step(g, carry):
        vf = v_scr[g]
        vb = vf.astype(BF16)
        y_intra = _dot(vb, m_ref[g])
        w = _dot(vb, g_ref[g])
        x0 = x0_ref[0, g]
        da = da_ref[g]
        db = db_ref[g]
        inj = cmul(x0, da[0:1], db[0:1])
        if seg == 1:
            w = w + inj
            x_in = x0
        else:
            w = w + jnp.where(row == 0, inj, 0.0)
            ws = pltpu.roll(w, half, 1)
            for kk in range(levels):
                s = 1 << kk
                a = da[kk:kk + 1]
                b = db[kk:kk + 1]
                if s % SUBLANES == 0:
                    sh, shs = w[:r - s], ws[:r - s]
                    w, ws = (jnp.concatenate([w[:s], w[s:] + (sh * a + shs * b)], axis=0),
                             jnp.concatenate([ws[:s], ws[s:] + (shs * a - sh * b)], axis=0))
                else:
                    sh = jnp.where(row >= s, pltpu.roll(w, s, 0), 0.0)
                    shs = jnp.where(row >= s, pltpu.roll(ws, s, 0), 0.0)
                    w, ws = w + (sh * a + shs * b), ws + (shs * a - sh * b)
            x_in = jnp.where(row == 0, x0, pltpu.roll(w, 1, 0))
        xf_ref[0, g] = w if seg == 1 else w[r - 1:r, :]
        y_scr[g] = y_intra + _dot(x_in.astype(BF16), h_ref[g]) + vf * dsk_ref[g]
        return carry

    lax.fori_loop(0, S5_GROUPS, group_step, 0, unroll=S5_GROUP_UNROLL)

    for vc in range(S5_LANE_CHUNKS):
        for col in range(n_col):
            for r0 in range(0, r, rp):
                ys = [y_scr[vc * SLOTS + kk, r0:r0 + rp, col * LANES:(col + 1) * LANES] for kk in range(SLOTS)]
                for tt in range(SLOTS):
                    acc = ys[(-tt) % SLOTS]
                    for sg in range(1, SLOTS):
                        acc = jnp.where(masks[sg], ys[(sg - tt) % SLOTS], acc)
                    if tt:
                        acc = pltpu.roll(acc, LANES - S5_GROUP * tt, 1)
                    y_ref[vc, pl.ds(col * SLOTS + tt + S5_CHUNK * r0, rp, stride=S5_CHUNK), :] = acc


def _s5(u, x0, ops, layer, seg_is_row):
    nb = x0.shape[0]
    n_tok = u.shape[1] // nb
    rows = n_tok // S5_CHUNK
    seg = 1 if seg_is_row else rows
    nseq = rows // seg
    levels = 0 if seg == 1 else (rows - 1).bit_length()
    assert rows % min(S5_FOLD_ROWS, rows) == 0 and ops[3].shape[2] >= max(levels, 1)
    tok = pl.BlockSpec((S5_LANE_CHUNKS, n_tok, LANES), lambda i: (0, i, 0))
    st = pl.BlockSpec((1, S5_GROUPS, nseq, 2 * S5_STATE), lambda i: (i, 0, 0, 0))
    return pl.pallas_call(
        functools.partial(_s5_body, rows=rows, seg=seg, levels=levels),
        grid=(nb,),
        in_specs=[tok, st] + [_layer_resident(o.shape, layer) for o in ops],
        out_specs=[tok, st],
        out_shape=[jax.ShapeDtypeStruct(u.shape, F32),
                   jax.ShapeDtypeStruct((nb, S5_GROUPS, nseq, 2 * S5_STATE), F32)],
        scratch_shapes=[pltpu.VMEM((S5_GROUPS, rows, S5_COLS), F32),
                        pltpu.VMEM((S5_GROUPS, rows, S5_COLS), F32)],
        compiler_params=pltpu.CompilerParams(dimension_semantics=("arbitrary",), vmem_limit_bytes=VMEM_LIMIT),
        name="s5",
    )(u, x0, *ops)


def _gelu_tanh(x):
    return 0.5 * x * (1.0 + jnp.tanh(math.sqrt(2.0 / math.pi) * (x + 0.044715 * (x * x * x))))


def _post_body(h_ref, o_ref, g_ref, y_ref, ga_ref, gb_ref, p_ref, gh_ref, wro_ref, wglu_ref, wout_ref, gmlp_ref,
               wup_ref, wdown_ref, gple_ref, wpg_ref, wpp_ref, gfin_ref, out_ref, *, final):
    d = h_ref.shape[1]
    gated = []
    for hd in range(RET_HEADS):
        sl = slice(hd * RET_DV, (hd + 1) * RET_DV)
        gated.append((g_ref[:, sl].astype(F32) * _rms(o_ref[:, sl].astype(F32), gh_ref[:, sl])).astype(BF16))
    branch_a = _dot(jnp.concatenate(gated, axis=1), wro_ref[...])
    y = jnp.concatenate([y_ref[vc] for vc in range(y_ref.shape[0])], axis=1)
    glu = _dot(_gelu_tanh(y).astype(BF16), wglu_ref[...])
    branch_b = glu[:, :d] * _sigmoid(glu[:, d:])
    merged = ga_ref[...].astype(F32) * branch_a + gb_ref[...].astype(F32) * branch_b
    h = h_ref[...] + _dot(merged.astype(BF16), wout_ref[...])
    inv = _inv_rms(h)
    up = _dot((h * gmlp_ref[...]).astype(BF16), wup_ref[...])
    h = h + (inv * inv) * _dot(jnp.square(jnp.maximum(up, 0.0)).astype(BF16), wdown_ref[...])
    gate = _sigmoid(_inv_rms(h) * _dot((h * gple_ref[...]).astype(BF16), wpg_ref[...]))
    h = h + gate * _dot(p_ref[...].astype(BF16), wpp_ref[...])
    if final:
        h = _rms(h, gfin_ref[...])
    out_ref[...] = h


def _post(h, o, g, y, ga, gb, p, layer, w, g_final, final):
    t, d = h.shape
    tm = min(POST_TOKEN_BLOCK, t)
    assert t % tm == 0
    tok = lambda wd: pl.BlockSpec((tm, wd), lambda i: (i, 0))
    names = ('g_ret_head', 'w_ret_o', 'w_glu', 'w_out', 'g_mlp', 'w_up', 'w_down', 'g_ple', 'w_ple_gate',
             'w_ple_proj')
    return pl.pallas_call(
        functools.partial(_post_body, final=final),
        grid=(t // tm,),
        in_specs=[tok(d), tok(V_W), tok(V_W), pl.BlockSpec((y.shape[0], tm, LANES), lambda i: (0, i, 0)), tok(d),
                  tok(d), pl.BlockSpec((None, tm, p.shape[2]), lambda i: (layer, i, 0))]
                 + [_layer_resident(w[n_].shape, layer) for n_ in names] + [_resident(g_final.shape)],
        out_specs=tok(d),
        out_shape=jax.ShapeDtypeStruct((t, d), F32),
        compiler_params=pltpu.CompilerParams(dimension_semantics=("arbitrary",), vmem_limit_bytes=VMEM_LIMIT),
        name="post",
    )(h, o, g, y, ga, gb, p, *[w[n_] for n_ in names], g_final)


def _rope_tables(pos, n_rows):
    half = RET_DK // 2
    inv = ROPE_BASE ** (-jnp.arange(half, dtype=F32) / half)
    ang = pos.astype(F32)[:, None] * inv[None, :]
    cos = jnp.cos(ang)
    sin = jnp.sin(ang)
    reps = max(1, n_rows // pos.shape[0])
    return (jnp.tile(jnp.concatenate([cos, cos], axis=1), (reps, 1)),
            jnp.tile(jnp.concatenate([-sin, sin], axis=1), (reps, 1)))


def _run_trunk(x, p, pos, s_ret0, s5_re0, s5_im0, w, s5_ops, g_final):
    b, l, d = x.shape
    t = b * l
    depth = w['w_in'].shape[0]
    cos_t, sin_t = _rope_tables(pos, min(TOKEN_BLOCK, t))
    seg_is_row = l == S5_CHUNK
    assert seg_is_row or l % (8 * S5_CHUNK) == 0
    if s5_re0 is None:
        x0_all = jnp.zeros((depth, b, S5_GROUPS, 2 * S5_STATE), F32)
    else:
        x0_all = jnp.concatenate([s5_re0, s5_im0], axis=-1)
    p = p.reshape(depth, t, -1)
    h = x.reshape(t, d)
    rets, s5s = [], []
    for i in range(depth):
        q, k, v, g, u, ga, gb = _in_proj(h, i, w, cos_t, sin_t, l)
        o, s_ret = _retention(q.reshape(b, l, Q_W), k.reshape(b, l, Q_W), v.reshape(b, l, V_W), s_ret0, i)
        if seg_is_row:
            y, xf = _s5(u, x0_all[i].transpose(1, 0, 2)[None], s5_ops, i, True)
            xf = xf[0].transpose(1, 0, 2)
        else:
            y, xf = _s5(u, x0_all[i][:, :, None, :], s5_ops, i, False)
            xf = xf[:, :, 0, :]
        h = _post(h, o.reshape(t, V_W), g, y, ga, gb, p, i, w, g_final, final=(i == depth - 1))
        rets.append(s_ret)
        s5s.append(xf)
    s5s = jnp.stack(s5s)
    return h.reshape(b, l, d), jnp.stack(rets), s5s[..., :S5_STATE], s5s[..., S5_STATE:]


def _prepare_weights(g_mix, w_in, g_ret_head, w_ret_o, w_glu_a, w_glu_b, w_out, g_mlp, w_up, w_down, g_ple,
                     w_ple_gate, w_ple_proj):
    row = lambda g: g.reshape(g.shape[0], 1, g.shape[1])
    return {
        'g_mix': row(g_mix), 'w_in': w_in.astype(BF16), 'g_ret_head': row(g_ret_head),
        'w_ret_o': w_ret_o.astype(BF16),
        'w_glu': jnp.concatenate([w_glu_a.astype(BF16), w_glu_b.astype(BF16)], axis=2),
        'w_out': w_out.astype(BF16), 'g_mlp': row(g_mlp), 'w_up': w_up.astype(BF16),
        'w_down': w_down.astype(BF16), 'g_ple': row(g_ple), 'w_ple_gate': w_ple_gate.astype(BF16),
        'w_ple_proj': w_ple_proj.astype(BF16),
    }


def kernel(x_prompt, x_sample, state_ret, state_s5_re, state_s5_im, p_prompt, p_sample, g_mix, w_in, g_ret_head, w_ret_o, s5_a_re, s5_a_im, s5_log_dt, s5_b_re, s5_b_im, s5_c_re, s5_c_im, s5_d, w_glu_a, w_glu_b, w_out, g_mlp, w_up, w_down, g_ple, w_ple_gate, w_ple_proj, g_final):
    d = x_prompt.shape[-1]
    w = _prepare_weights(g_mix, w_in, g_ret_head, w_ret_o, w_glu_a, w_glu_b, w_out, g_mlp, w_up, w_down, g_ple,
                         w_ple_gate, w_ple_proj)
    max_rows = max(x_prompt.shape[1], x_sample.shape[1]) // S5_CHUNK
    s5_ops = _s5_operators(s5_a_re, s5_a_im, s5_log_dt, s5_b_re, s5_b_im, s5_c_re, s5_c_im, s5_d,
                           (max_rows - 1).bit_length())
    g_fin = g_final.reshape(1, d)
    pos_prompt = jnp.arange(x_prompt.shape[1])
    pos_sample = PAST_LEN + jnp.arange(x_sample.shape[1])
    y_p, ret_p, s5re_p, s5im_p = _run_trunk(x_prompt, p_prompt, pos_prompt, None, None, None, w, s5_ops, g_fin)
    y_s, ret_s, s5re_s, s5im_s = _run_trunk(x_sample, p_sample, pos_sample, state_ret, state_s5_re, state_s5_im,
                                            w, s5_ops, g_fin)
    return (y_p, y_s, ret_p, s5re_p, s5im_p, ret_s, s5re_s, s5im_s)
```

```python
import functools
import math

import jax
import jax.numpy as jnp
import numpy as np
from jax import lax
from jax.experimental import pallas as pl
from jax.experimental.pallas import tpu as pltpu

F32 = jnp.float32
BF16 = jnp.bfloat16
HIGHEST = lax.Precision.HIGHEST

EPS = 1e-6
ROPE_BASE = 10000.0
PAST_LEN = 2048
RET_HEADS = 4
RET_DK = 128
RET_DV = 256
Q_W = RET_HEADS * RET_DK
V_W = RET_HEADS * RET_DV
S5_GROUP = 16
S5_GROUPS = 32
S5_WIDTH = S5_GROUP * S5_GROUPS
S5_STATE = 64
S5_CHUNK = 16
S5_COLS = S5_CHUNK * S5_GROUP
LANES = 128
SUBLANES = 8
SLOTS = LANES // S5_GROUP
S5_LANE_CHUNKS = S5_WIDTH // LANES
S5_FOLD_ROWS = 32
S5_GROUP_UNROLL = 4
VMEM_LIMIT = 56 * 1024 * 1024

RET_CHUNK = 256
RET_CHUNKS_PER_STEP = 4
RET_SHORT_SEQS = 8
TOKEN_BLOCK = 512
POST_TOKEN_BLOCK = 512
POST_SUB_ROWS = 256


def _resident(shape):
    nd = len(shape)
    return pl.BlockSpec(shape, lambda *_: (0,) * nd, pipeline_mode=pl.Buffered(1))


def _layer_resident(shape, layer):
    nd = len(shape)
    return pl.BlockSpec((None,) + tuple(shape[1:]), lambda *_: (layer,) + (0,) * (nd - 1),
                        pipeline_mode=pl.Buffered(1))


def _sigmoid(x):
    return 1.0 / (1.0 + jnp.exp(-x))


def _inv_rms(x):
    return lax.rsqrt(jnp.mean(x * x, axis=-1, keepdims=True) + EPS)


def _rms(x, g):
    return (x * _inv_rms(x)) * g


def _dot(a, b):
    return jnp.dot(a, b, preferred_element_type=F32)


def _in_proj_body(x_ref, gm_ref, w_ref, cos_ref, sin_ref, q_ref, k_ref, v_ref, g_ref, u_ref, ga_ref, gb_ref):
    x = x_ref[...]
    inv = _inv_rms(x)
    xg = (x * gm_ref[...]).astype(BF16)
    cos = cos_ref[...]
    sin = sin_ref[...]
    d = x_ref.shape[1]

    def proj(lo, hi):
        return inv * _dot(xg, w_ref[:, lo:hi])

    def rope(z, hd):
        xh = z[:, hd * RET_DK:(hd + 1) * RET_DK]
        return xh * cos + pltpu.roll(xh, RET_DK // 2, 1) * sin

    o_v = 2 * Q_W
    o_u = o_v + 2 * V_W
    o_gate = o_u + S5_WIDTH
    ga_ref[...] = _sigmoid(proj(o_gate, o_gate + d)).astype(BF16)
    gb_ref[...] = _sigmoid(proj(o_gate + d, o_gate + 2 * d)).astype(BF16)
    zg = proj(o_v + V_W, o_v + 2 * V_W)
    g_ref[...] = (zg * _sigmoid(zg)).astype(BF16)
    zq = proj(0, Q_W)
    zk = proj(Q_W, 2 * Q_W)
    for hd in range(RET_HEADS):
        sl = slice(hd * RET_DK, (hd + 1) * RET_DK)
        q_ref[:, sl] = rope(zq, hd).astype(BF16)
        k_ref[:, sl] = (rope(zk, hd) * (RET_DK ** -0.5)).astype(BF16)
    v_ref[...] = proj(o_v, o_v + V_W).astype(BF16)
    zu = proj(o_u, o_u + S5_WIDTH)
    for vc in range(S5_LANE_CHUNKS):
        u_ref[vc] = zu[:, vc * LANES:(vc + 1) * LANES]


def _in_proj(h, layer, w, cos_t, sin_t, seq_len):
    t, d = h.shape
    tm = min(TOKEN_BLOCK, t)
    assert t % tm == 0 and (seq_len % tm == 0 or tm % seq_len == 0)
    n_pos_blocks = cos_t.shape[0] // tm
    tok = lambda wd: pl.BlockSpec((tm, wd), lambda i: (i, 0))
    pos = pl.BlockSpec((tm, RET_DK), lambda i: (i % n_pos_blocks, 0))
    bf = lambda wd: (tok(wd), jax.ShapeDtypeStruct((t, wd), BF16))
    u_out = (pl.BlockSpec((S5_LANE_CHUNKS, tm, LANES), lambda i: (0, i, 0)),
             jax.ShapeDtypeStruct((S5_LANE_CHUNKS, t, LANES), F32))
    outs = (bf(Q_W), bf(Q_W), bf(V_W), bf(V_W), u_out, bf(d), bf(d))
    return pl.pallas_call(
        _in_proj_body,
        grid=(t // tm,),
        in_specs=[tok(d), _layer_resident(w['g_mix'].shape, layer), _layer_resident(w['w_in'].shape, layer),
                  pos, pos],
        out_specs=[o[0] for o in outs],
        out_shape=[o[1] for o in outs],
        compiler_params=pltpu.CompilerParams(dimension_semantics=("arbitrary",), vmem_limit_bytes=VMEM_LIMIT),
        name="in_proj",
    )(h, w['g_mix'], w['w_in'], cos_t, sin_t)


def _retention_body(*refs, chunk, n_chunks, n_seq, zero_init):
    if zero_init:
        q_ref, k_ref, v_ref, o_ref, s_ref, inner_scr, qd_scr, kd_scr = refs
    else:
        q_ref, k_ref, v_ref, s0_ref, o_ref, s_ref, inner_scr, qd_scr, kd_scr = refs
    c = chunk

    @pl.when(pl.program_id(1) == 0)
    def _():
        s_ref[...] = jnp.zeros(s_ref.shape, F32) if zero_init else s0_ref[...]
        row_cc = lax.broadcasted_iota(jnp.int32, (c, c), 0)
        col_cc = lax.broadcasted_iota(jnp.int32, (c, c), 1)
        diff = (row_cc - col_cc).astype(F32)
        idx_v = lax.broadcasted_iota(jnp.int32, (c, RET_DV), 0).astype(F32)
        idx_k = lax.broadcasted_iota(jnp.int32, (c, RET_DK), 0).astype(F32)
        for hd in range(RET_HEADS):
            lg = math.log1p(-(2.0 ** (-5.0 - hd)))
            inner_scr[hd] = jnp.where(diff >= 0.0, jnp.exp(jnp.maximum(diff, 0.0) * lg), 0.0)
            qd_scr[hd] = jnp.exp((idx_v + 1.0) * lg)
            kd_scr[hd] = jnp.exp((c - 1.0 - idx_k) * lg)

    for sq in range(n_seq):
        for ci in range(n_chunks):
            rows = slice(ci * c, (ci + 1) * c)
            for hd in range(RET_HEADS):
                lg = math.log1p(-(2.0 ** (-5.0 - hd)))
                q = q_ref[sq, rows, hd * RET_DK:(hd + 1) * RET_DK]
                k = k_ref[sq, rows, hd * RET_DK:(hd + 1) * RET_DK]
                v = v_ref[sq, rows, hd * RET_DV:(hd + 1) * RET_DV]
                s = s_ref[sq, hd]
                scores = lax.dot_general(q, k, (((1,), (1,)), ((), ())), preferred_element_type=F32)
                o = _dot((scores * inner_scr[hd]).astype(BF16), v) + qd_scr[hd] * _dot(q, s.astype(BF16))
                kd = (k.astype(F32) * kd_scr[hd]).astype(BF16)
                s_ref[sq, hd] = math.exp(c * lg) * s + lax.dot_general(
                    kd, v, (((0,), (0,)), ((), ())), preferred_element_type=F32)
                o_ref[sq, rows, hd * RET_DV:(hd + 1) * RET_DV] = o.astype(BF16)


def _retention(q, k, v, s0, layer):
    b, l, _ = q.shape
    c = min(RET_CHUNK, l)
    n_chunks = min(RET_CHUNKS_PER_STEP, l // c)
    tl = c * n_chunks
    assert l % tl == 0
    n_seq = RET_SHORT_SEQS if (l == c and b % RET_SHORT_SEQS == 0) else 1
    seq = lambda wd: pl.BlockSpec((n_seq, tl, wd), lambda i, j: (i, j, 0))
    st = pl.BlockSpec((n_seq, RET_HEADS, RET_DK, RET_DV), lambda i, j: (i, 0, 0, 0))
    st_in = pl.BlockSpec((None, n_seq, RET_HEADS, RET_DK, RET_DV), lambda i, j: (layer, i, 0, 0, 0))
    zero_init = s0 is None
    return pl.pallas_call(
        functools.partial(_retention_body, chunk=c, n_chunks=n_chunks, n_seq=n_seq, zero_init=zero_init),
        grid=(b // n_seq, l // tl),
        in_specs=[seq(Q_W), seq(Q_W), seq(V_W)] + ([] if zero_init else [st_in]),
        out_specs=[seq(V_W), st],
        out_shape=[jax.ShapeDtypeStruct((b, l, V_W), BF16),
                   jax.ShapeDtypeStruct((b, RET_HEADS, RET_DK, RET_DV), F32)],
        scratch_shapes=[pltpu.VMEM((RET_HEADS, c, c), F32), pltpu.VMEM((RET_HEADS, c, RET_DV), F32),
                        pltpu.VMEM((RET_HEADS, c, RET_DK), F32)],
        compiler_params=pltpu.CompilerParams(dimension_semantics=("arbitrary", "arbitrary"),
                                             vmem_limit_bytes=VMEM_LIMIT),
        name="retention",
    )(q, k, v, *([] if zero_init else [s0]))


def _slot_permutation():
    sig = np.arange(S5_CHUNK)
    grp = np.arange(S5_GROUPS)
    t_of = SLOTS * (sig[None, :] // SLOTS) + (sig[None, :] % SLOTS - grp[:, None]) % SLOTS
    perm = (t_of[:, :, None] * S5_GROUP + np.arange(S5_GROUP)[None, None, :]).reshape(S5_GROUPS, S5_COLS)
    return (jnp.arange(S5_COLS, dtype=jnp.int32)[None, :, None] == jnp.asarray(perm, jnp.int32)[:, None, :]
            ).astype(BF16)


def _s5_operators(a_re, a_im, log_dt, b_re, b_im, c_re, c_im, d_skip, levels):
    n_pos = S5_CHUNK
    n_l = a_re.shape[0]
    dt = jnp.exp(log_dt)[..., None]
    mag = jnp.exp(a_re * dt)
    lr = mag * jnp.cos(a_im * dt)
    li = mag * jnp.sin(a_im * dt)
    den = a_re * a_re + a_im * a_im
    fr = ((lr - 1.0) * a_re + li * a_im) / den
    fi = (li * a_re - (lr - 1.0) * a_im) / den
    bb_re = fr[..., None] * b_re - fi[..., None] * b_im
    bb_im = fr[..., None] * b_im + fi[..., None] * b_re
    bt_re = bb_re.transpose(0, 1, 3, 2)
    bt_im = bb_im.transpose(0, 1, 3, 2)

    def lam_pow(kk):
        kk = jnp.asarray(kk, F32)[None, None, :, None]
        m = jnp.exp(kk * (a_re * dt)[:, :, None])
        return m * jnp.cos(kk * (a_im * dt)[:, :, None]), m * jnp.sin(kk * (a_im * dt)[:, :, None])

    pr, pi = lam_pow(np.arange(n_pos + 1))
    cp_re = c_re[:, :, None] * pr[:, :, :n_pos, None] - c_im[:, :, None] * pi[:, :, :n_pos, None]
    cp_im = c_re[:, :, None] * pi[:, :, :n_pos, None] + c_im[:, :, None] * pr[:, :, :n_pos, None]
    cp_cat = jnp.concatenate([cp_re, -cp_im], axis=-1).reshape(n_l, S5_GROUPS, S5_COLS, 2 * S5_STATE)
    bt_cat = jnp.concatenate([bt_re, bt_im], axis=-1)
    kcat = jnp.einsum('lgcn,lgxn->lgcx', bt_cat, cp_cat, precision=HIGHEST)
    kpad = jnp.pad(kcat, ((0, 0), (0, 0), (0, 0), (S5_COLS, 0)))
    m_nat = jnp.concatenate(
        [kpad[..., S5_COLS - S5_GROUP * s:2 * S5_COLS - S5_GROUP * s] for s in range(n_pos)], axis=2)
    rr = pr[:, :, n_pos - 1::-1][:, :, :, None]
    ri = pi[:, :, n_pos - 1::-1][:, :, :, None]
    g_re = rr * bt_re[:, :, None] - ri * bt_im[:, :, None]
    g_im = rr * bt_im[:, :, None] + ri * bt_re[:, :, None]
    g_nat = jnp.concatenate([g_re, g_im], axis=-1).reshape(n_l, S5_GROUPS, S5_COLS, 2 * S5_STATE)
    pt_re = pr[:, :, 1:].transpose(0, 1, 3, 2)[..., None]
    pt_im = pi[:, :, 1:].transpose(0, 1, 3, 2)[..., None]
    ct_re = c_re.transpose(0, 1, 3, 2)[:, :, :, None]
    ct_im = c_im.transpose(0, 1, 3, 2)[:, :, :, None]
    hp_re = ct_re * pt_re - ct_im * pt_im
    hp_im = ct_re * pt_im + ct_im * pt_re
    h_nat = jnp.concatenate([hp_re, -hp_im], axis=2).reshape(n_l, S5_GROUPS, 2 * S5_STATE, S5_COLS)
    perm = _slot_permutation()
    m_op = jnp.einsum('gia,lgib->lgab', perm, jnp.einsum('lgij,gjb->lgib', m_nat.astype(BF16), perm))
    g_op = jnp.einsum('gia,lgin->lgan', perm, g_nat.astype(BF16))
    h_op = jnp.einsum('lgnj,gjb->lgnb', h_nat.astype(BF16), perm)
    dr, di = lam_pow(n_pos * (2 ** np.arange(max(levels, 1))))
    d_a = jnp.concatenate([dr, dr], axis=-1)
    d_b = jnp.concatenate([-di, di], axis=-1)
    d_tile = jnp.tile(d_skip.reshape(n_l, S5_GROUPS, 1, S5_GROUP), (1, 1, S5_CHUNK, 1)).reshape(
        n_l, S5_GROUPS, 1, S5_COLS)
    return m_op, g_op, h_op, d_a, d_b, d_tile


def _s5_body(u_ref, x0_ref, m_ref, g_ref, h_ref, da_ref, db_ref, dsk_ref, y_ref, xf_ref, v_scr, y_scr,
             *, rows, seg, levels):
    r = rows
    half = S5_STATE
    rp = min(S5_FOLD_ROWS, r)
    slot = lax.broadcasted_iota(jnp.int32, (rp, LANES), 1) // S5_GROUP
    masks = [slot == s for s in range(SLOTS)]
    n_col = S5_CHUNK // SLOTS

    for vc in range(S5_LANE_CHUNKS):
        for col in range(n_col):
            for r0 in range(0, r, rp):
                srcs = []
                for tt in range(SLOTS):
                    a = u_ref[vc, pl.ds(col * SLOTS + tt + S5_CHUNK * r0, rp, stride=S5_CHUNK), :]
                    srcs.append(pltpu.roll(a, S5_GROUP * tt, 1) if tt else a)
                for kk in range(SLOTS):
                    acc = srcs[0]
                    for tt in range(1, SLOTS):
                        acc = jnp.where(masks[(kk + tt) % SLOTS], srcs[tt], acc)
                    v_scr[vc * SLOTS + kk, r0:r0 + rp, col * LANES:(col + 1) * LANES] = acc

    row = lax.broadcasted_iota(jnp.int32, (r, 2 * half), 0)

    def cmul(x, a, b):
        return x * a + pltpu.roll(x, half, 1) * b

    def group_step(g, carry):
        vf = v_scr[g]
        vb = vf.astype(BF16)
        y_intra = _dot(vb, m_ref[g])
        w = _dot(vb, g_ref[g])
        x0 = x0_ref[0, g]
        da = da_ref[g]
        db = db_ref[g]
        inj = cmul(x0, da[0:1], db[0:1])
        if seg == 1:
            w = w + inj
            x_in = x0
        else:
            w = w + jnp.where(row == 0, inj, 0.0)
            ws = pltpu.roll(w, half, 1)
            for kk in range(levels):
                s = 1 << kk
                a = da[kk:kk + 1]
                b = db[kk:kk + 1]
                if s % SUBLANES == 0:
                    sh, shs = w[:r - s], ws[:r - s]
                    w, ws = (jnp.concatenate([w[:s], w[s:] + (sh * a + shs * b)], axis=0),
                             jnp.concatenate([ws[:s], ws[s:] + (shs * a - sh * b)], axis=0))
                else:
                    sh = jnp.where(row >= s, pltpu.roll(w, s, 0), 0.0)
                    shs = jnp.where(row >= s, pltpu.roll(ws, s, 0), 0.0)
                    w, ws = w + (sh * a + shs * b), ws + (shs * a - sh * b)
            x_in = jnp.where(row == 0, x0, pltpu.roll(w, 1, 0))
        xf_ref[0, g] = w if seg == 1 else w[r - 1:r, :]
        y_scr[g] = y_intra + _dot(x_in.astype(BF16), h_ref[g]) + vf * dsk_ref[g]
        return carry

    lax.fori_loop(0, S5_GROUPS, group_step, 0, unroll=S5_GROUP_UNROLL)

    for vc in range(S5_LANE_CHUNKS):
        for col in range(n_col):
            for r0 in range(0, r, rp):
                ys = [y_scr[vc * SLOTS + kk, r0:r0 + rp, col * LANES:(col + 1) * LANES] for kk in range(SLOTS)]
                for tt in range(SLOTS):
                    acc = ys[(-tt) % SLOTS]
                    for sg in range(1, SLOTS):
                        acc = jnp.where(masks[sg], ys[(sg - tt) % SLOTS], acc)
                    if tt:
                        acc = pltpu.roll(acc, LANES - S5_GROUP * tt, 1)
                    y_ref[vc, pl.ds(col * SLOTS + tt + S5_CHUNK * r0, rp, stride=S5_CHUNK), :] = acc


def _s5(u, x0, ops, layer, seg_is_row):
    nb = x0.shape[0]
    n_tok = u.shape[1] // nb
    rows = n_tok // S5_CHUNK
    seg = 1 if seg_is_row else rows
    nseq = rows // seg
    levels = 0 if seg == 1 else (rows - 1).bit_length()
    assert rows % min(S5_FOLD_ROWS, rows) == 0 and ops[3].shape[2] >= max(levels, 1)
    tok = pl.BlockSpec((S5_LANE_CHUNKS, n_tok, LANES), lambda i: (0, i, 0))
    st = pl.BlockSpec((1, S5_GROUPS, nseq, 2 * S5_STATE), lambda i: (i, 0, 0, 0))
    return pl.pallas_call(
        functools.partial(_s5_body, rows=rows, seg=seg, levels=levels),
        grid=(nb,),
        in_specs=[tok, st] + [_layer_resident(o.shape, layer) for o in ops],
        out_specs=[tok, st],
        out_shape=[jax.ShapeDtypeStruct(u.shape, F32),
                   jax.ShapeDtypeStruct((nb, S5_GROUPS, nseq, 2 * S5_STATE), F32)],
        scratch_shapes=[pltpu.VMEM((S5_GROUPS, rows, S5_COLS), F32),
                        pltpu.VMEM((S5_GROUPS, rows, S5_COLS), F32)],
        compiler_params=pltpu.CompilerParams(dimension_semantics=("arbitrary",), vmem_limit_bytes=VMEM_LIMIT),
        name="s5",
    )(u, x0, *ops)


def _gelu_tanh(x):
    return 0.5 * x * (1.0 + jnp.tanh(math.sqrt(2.0 / math.pi) * (x + 0.044715 * (x * x * x))))


def _post_body(h_ref, o_ref, g_ref, y_ref, ga_ref, gb_ref, p_ref, gh_ref, wro_ref, wglu_ref, wout_ref, gmlp_ref,
               wup_ref, wdown_ref, gple_ref, wpg_ref, wpp_ref, gfin_ref, out_ref, *, final, sub_rows):
    d = h_ref.shape[1]

    def sub_tile(j, carry):
        rows = pl.ds(pl.multiple_of(j * sub_rows, sub_rows), sub_rows)
        gated = []
        for hd in range(RET_HEADS):
            sl = slice(hd * RET_DV, (hd + 1) * RET_DV)
            gated.append((g_ref[rows, sl].astype(F32)
                          * _rms(o_ref[rows, sl].astype(F32), gh_ref[:, sl])).astype(BF16))
        branch_a = _dot(jnp.concatenate(gated, axis=1), wro_ref[...])
        y = jnp.concatenate([y_ref[vc, rows, :] for vc in range(y_ref.shape[0])], axis=1)
        glu = _dot(_gelu_tanh(y).astype(BF16), wglu_ref[...])
        branch_b = glu[:, :d] * _sigmoid(glu[:, d:])
        merged = ga_ref[rows, :].astype(F32) * branch_a + gb_ref[rows, :].astype(F32) * branch_b
        h = h_ref[rows, :] + _dot(merged.astype(BF16), wout_ref[...])
        inv = _inv_rms(h)
        up = _dot((h * gmlp_ref[...]).astype(BF16), wup_ref[...])
        h = h + (inv * inv) * _dot(jnp.square(jnp.maximum(up, 0.0)).astype(BF16), wdown_ref[...])
        gate = _sigmoid(_inv_rms(h) * _dot((h * gple_ref[...]).astype(BF16), wpg_ref[...]))
        h = h + gate * _dot(p_ref[rows, :].astype(BF16), wpp_ref[...])
        if final:
            h = _rms(h, gfin_ref[...])
        out_ref[rows, :] = h
        return carry

    lax.fori_loop(0, h_ref.shape[0] // sub_rows, sub_tile, 0)


def _post(h, o, g, y, ga, gb, p, layer, w, g_final, final):
    t, d = h.shape
    sub_rows = min(POST_SUB_ROWS, t)
    tm = min(POST_TOKEN_BLOCK, t)
    assert t % tm == 0 and tm % sub_rows == 0
    tok = lambda wd: pl.BlockSpec((tm, wd), lambda i: (i, 0))
    names = ('g_ret_head', 'w_ret_o', 'w_glu', 'w_out', 'g_mlp', 'w_up', 'w_down', 'g_ple', 'w_ple_gate',
             'w_ple_proj')
    return pl.pallas_call(
        functools.partial(_post_body, final=final, sub_rows=sub_rows),
        grid=(t // tm,),
        in_specs=[tok(d), tok(V_W), tok(V_W), pl.BlockSpec((y.shape[0], tm, LANES), lambda i: (0, i, 0)), tok(d),
                  tok(d), pl.BlockSpec((None, tm, p.shape[2]), lambda i: (layer, i, 0))]
                 + [_layer_resident(w[n_].shape, layer) for n_ in names] + [_resident(g_final.shape)],
        out_specs=tok(d),
        out_shape=jax.ShapeDtypeStruct((t, d), F32),
        compiler_params=pltpu.CompilerParams(dimension_semantics=("arbitrary",), vmem_limit_bytes=VMEM_LIMIT),
        name="post",
    )(h, o, g, y, ga, gb, p, *[w[n_] for n_ in names], g_final)


def _rope_tables(pos, n_rows):
    half = RET_DK // 2
    inv = ROPE_BASE ** (-jnp.arange(half, dtype=F32) / half)
    ang = pos.astype(F32)[:, None] * inv[None, :]
    cos = jnp.cos(ang)
    sin = jnp.sin(ang)
    reps = max(1, n_rows // pos.shape[0])
    return (jnp.tile(jnp.concatenate([cos, cos], axis=1), (reps, 1)),
            jnp.tile(jnp.concatenate([-sin, sin], axis=1), (reps, 1)))


def _run_trunk(x, p, pos, s_ret0, s5_re0, s5_im0, w, s5_ops, g_final):
    b, l, d = x.shape
    t = b * l
    depth = w['w_in'].shape[0]
    cos_t, sin_t = _rope_tables(pos, min(TOKEN_BLOCK, t))
    seg_is_row = l == S5_CHUNK
    assert seg_is_row or l % (8 * S5_CHUNK) == 0
    if s5_re0 is None:
        x0_all = jnp.zeros((depth, b, S5_GROUPS, 2 * S5_STATE), F32)
    else:
        x0_all = jnp.concatenate([s5_re0, s5_im0], axis=-1)
    p = p.reshape(depth, t, -1)
    h = x.reshape(t, d)
    rets, s5s = [], []
    for i in range(depth):
        q, k, v, g, u, ga, gb = _in_proj(h, i, w, cos_t, sin_t, l)
        o, s_ret = _retention(q.reshape(b, l, Q_W), k.reshape(b, l, Q_W), v.reshape(b, l, V_W), s_ret0, i)
        if seg_is_row:
            y, xf = _s5(u, x0_all[i].transpose(1, 0, 2)[None], s5_ops, i, True)
            xf = xf[0].transpose(1, 0, 2)
        else:
            y, xf = _s5(u, x0_all[i][:, :, None, :], s5_ops, i, False)
            xf = xf[:, :, 0, :]
        h = _post(h, o.reshape(t, V_W), g, y, ga, gb, p, i, w, g_final, final=(i == depth - 1))
        rets.append(s_ret)
        s5s.append(xf)
    s5s = jnp.stack(s5s)
    return h.reshape(b, l, d), jnp.stack(rets), s5s[..., :S5_STATE], s5s[..., S5_STATE:]


def _prepare_weights(g_mix, w_in, g_ret_head, w_ret_o, w_glu_a, w_glu_b, w_out, g_mlp, w_up, w_down, g_ple,
                     w_ple_gate, w_ple_proj):
    row = lambda g: g.reshape(g.shape[0], 1, g.shape[1])
    return {
        'g_mix': row(g_mix), 'w_in': w_in.astype(BF16), 'g_ret_head': row(g_ret_head),
        'w_ret_o': w_ret_o.astype(BF16),
        'w_glu': jnp.concatenate([w_glu_a.astype(BF16), w_glu_b.astype(BF16)], axis=2),
        'w_out': w_out.astype(BF16), 'g_mlp': row(g_mlp), 'w_up': w_up.astype(BF16),
        'w_down': w_down.astype(BF16), 'g_ple': row(g_ple), 'w_ple_gate': w_ple_gate.astype(BF16),
        'w_ple_proj': w_ple_proj.astype(BF16),
    }


def kernel(x_prompt, x_sample, state_ret, state_s5_re, state_s5_im, p_prompt, p_sample, g_mix, w_in, g_ret_head, w_ret_o, s5_a_re, s5_a_im, s5_log_dt, s5_b_re, s5_b_im, s5_c_re, s5_c_im, s5_d, w_glu_a, w_glu_b, w_out, g_mlp, w_up, w_down, g_ple, w_ple_gate, w_ple_proj, g_final):
    d = x_prompt.shape[-1]
    w = _prepare_weights(g_mix, w_in, g_ret_head, w_ret_o, w_glu_a, w_glu_b, w_out, g_mlp, w_up, w_down, g_ple,
                         w_ple_gate, w_ple_proj)
    max_rows = max(x_prompt.shape[1], x_sample.shape[1]) // S5_CHUNK
    s5_ops = _s5_operators(s5_a_re, s5_a_im, s5_log_dt, s5_b_re, s5_b_im, s5_c_re, s5_c_im, s5_d,
                           (max_rows - 1).bit_length())
    g_fin = g_final.reshape(1, d)
    pos_prompt = jnp.arange(x_prompt.shape[1])
    pos_sample = PAST_LEN + jnp.arange(x_sample.shape[1])
    y_p, ret_p, s5re_p, s5im_p = _run_trunk(x_prompt, p_prompt, pos_prompt, None, None, None, w, s5_ops, g_fin)
    y_s, ret_s, s5re_s, s5im_s = _run_trunk(x_sample, p_sample, pos_sample, state_ret, state_s5_re, state_s5_im,
                                            w, s5_ops, g_fin)
    return (y_p, y_s, ret_p, s5re_p, s5im_p, ret_s, s5re_s, s5im_s)
```

```python
import functools
import math

import jax
import jax.numpy as jnp
import numpy as np
from jax import lax
from jax.experimental import pallas as pl
from jax.experimental.pallas import tpu as pltpu

F32 = jnp.float32
BF16 = jnp.bfloat16
HIGHEST = lax.Precision.HIGHEST

EPS = 1e-6
ROPE_BASE = 10000.0
PAST_LEN = 2048
RET_HEADS = 4
RET_DK = 128
RET_DV = 256
Q_W = RET_HEADS * RET_DK
V_W = RET_HEADS * RET_DV
S5_GROUP = 16
S5_GROUPS = 32
S5_WIDTH = S5_GROUP * S5_GROUPS
S5_STATE = 64
S5_CHUNK = 16
S5_COLS = S5_CHUNK * S5_GROUP
LANES = 128
SUBLANES = 8
SLOTS = LANES // S5_GROUP
S5_LANE_CHUNKS = S5_WIDTH // LANES
S5_FOLD_ROWS = 32
S5_LONG_SEQS = 2
S5_GROUP_UNROLL = 4
VMEM_LIMIT = 56 * 1024 * 1024

RET_CHUNK = 256
RET_CHUNKS_PER_STEP = 4
RET_SHORT_SEQS = 8
TOKEN_BLOCK = 512
POST_TOKEN_BLOCK = 512
POST_SUB_ROWS = 256


def _resident(shape):
    nd = len(shape)
    return pl.BlockSpec(shape, lambda *_: (0,) * nd, pipeline_mode=pl.Buffered(1))


def _layer_resident(shape, layer):
    nd = len(shape)
    return pl.BlockSpec((None,) + tuple(shape[1:]), lambda *_: (layer,) + (0,) * (nd - 1),
                        pipeline_mode=pl.Buffered(1))


def _sigmoid(x):
    return 1.0 / (1.0 + jnp.exp(-x))


def _inv_rms(x):
    return lax.rsqrt(jnp.mean(x * x, axis=-1, keepdims=True) + EPS)


def _rms(x, g):
    return (x * _inv_rms(x)) * g


def _dot(a, b):
    return jnp.dot(a, b, preferred_element_type=F32)


def _in_proj_body(x_ref, gm_ref, w_ref, cos_ref, sin_ref, q_ref, k_ref, v_ref, g_ref, u_ref, ga_ref, gb_ref):
    x = x_ref[...]
    inv = _inv_rms(x)
    xg = (x * gm_ref[...]).astype(BF16)
    cos = cos_ref[...]
    sin = sin_ref[...]
    d = x_ref.shape[1]

    def proj(lo, hi):
        return inv * _dot(xg, w_ref[:, lo:hi])

    def rope(z, hd):
        xh = z[:, hd * RET_DK:(hd + 1) * RET_DK]
        return xh * cos + pltpu.roll(xh, RET_DK // 2, 1) * sin

    o_v = 2 * Q_W
    o_u = o_v + 2 * V_W
    o_gate = o_u + S5_WIDTH
    ga_ref[...] = _sigmoid(proj(o_gate, o_gate + d)).astype(BF16)
    gb_ref[...] = _sigmoid(proj(o_gate + d, o_gate + 2 * d)).astype(BF16)
    zg = proj(o_v + V_W, o_v + 2 * V_W)
    g_ref[...] = (zg * _sigmoid(zg)).astype(BF16)
    zq = proj(0, Q_W)
    zk = proj(Q_W, 2 * Q_W)
    for hd in range(RET_HEADS):
        sl = slice(hd * RET_DK, (hd + 1) * RET_DK)
        q_ref[:, sl] = rope(zq, hd).astype(BF16)
        k_ref[:, sl] = (rope(zk, hd) * (RET_DK ** -0.5)).astype(BF16)
    v_ref[...] = proj(o_v, o_v + V_W).astype(BF16)
    zu = proj(o_u, o_u + S5_WIDTH)
    for vc in range(S5_LANE_CHUNKS):
        u_ref[vc] = zu[:, vc * LANES:(vc + 1) * LANES]


def _in_proj(h, layer, w, cos_t, sin_t, seq_len):
    t, d = h.shape
    tm = min(TOKEN_BLOCK, t)
    assert t % tm == 0 and (seq_len % tm == 0 or tm % seq_len == 0)
    n_pos_blocks = cos_t.shape[0] // tm
    tok = lambda wd: pl.BlockSpec((tm, wd), lambda i: (i, 0))
    pos = pl.BlockSpec((tm, RET_DK), lambda i: (i % n_pos_blocks, 0))
    bf = lambda wd: (tok(wd), jax.ShapeDtypeStruct((t, wd), BF16))
    u_out = (pl.BlockSpec((S5_LANE_CHUNKS, tm, LANES), lambda i: (0, i, 0)),
             jax.ShapeDtypeStruct((S5_LANE_CHUNKS, t, LANES), F32))
    outs = (bf(Q_W), bf(Q_W), bf(V_W), bf(V_W), u_out, bf(d), bf(d))
    return pl.pallas_call(
        _in_proj_body,
        grid=(t // tm,),
        in_specs=[tok(d), _layer_resident(w['g_mix'].shape, layer), _layer_resident(w['w_in'].shape, layer),
                  pos, pos],
        out_specs=[o[0] for o in outs],
        out_shape=[o[1] for o in outs],
        compiler_params=pltpu.CompilerParams(dimension_semantics=("arbitrary",), vmem_limit_bytes=VMEM_LIMIT),
        name="in_proj",
    )(h, w['g_mix'], w['w_in'], cos_t, sin_t)


def _retention_body(*refs, chunk, n_chunks, n_seq, zero_init):
    if zero_init:
        q_ref, k_ref, v_ref, o_ref, s_ref, inner_scr, qd_scr, kd_scr = refs
    else:
        q_ref, k_ref, v_ref, s0_ref, o_ref, s_ref, inner_scr, qd_scr, kd_scr = refs
    c = chunk

    @pl.when(pl.program_id(1) == 0)
    def _():
        s_ref[...] = jnp.zeros(s_ref.shape, F32) if zero_init else s0_ref[...]
        row_cc = lax.broadcasted_iota(jnp.int32, (c, c), 0)
        col_cc = lax.broadcasted_iota(jnp.int32, (c, c), 1)
        diff = (row_cc - col_cc).astype(F32)
        idx_v = lax.broadcasted_iota(jnp.int32, (c, RET_DV), 0).astype(F32)
        idx_k = lax.broadcasted_iota(jnp.int32, (c, RET_DK), 0).astype(F32)
        for hd in range(RET_HEADS):
            lg = math.log1p(-(2.0 ** (-5.0 - hd)))
            inner_scr[hd] = jnp.where(diff >= 0.0, jnp.exp(jnp.maximum(diff, 0.0) * lg), 0.0)
            qd_scr[hd] = jnp.exp((idx_v + 1.0) * lg)
            kd_scr[hd] = jnp.exp((c - 1.0 - idx_k) * lg)

    for sq in range(n_seq):
        for ci in range(n_chunks):
            rows = slice(ci * c, (ci + 1) * c)
            for hd in range(RET_HEADS):
                lg = math.log1p(-(2.0 ** (-5.0 - hd)))
                q = q_ref[sq, rows, hd * RET_DK:(hd + 1) * RET_DK]
                k = k_ref[sq, rows, hd * RET_DK:(hd + 1) * RET_DK]
                v = v_ref[sq, rows, hd * RET_DV:(hd + 1) * RET_DV]
                s = s_ref[sq, hd]
                scores = lax.dot_general(q, k, (((1,), (1,)), ((), ())), preferred_element_type=F32)
                o = _dot((scores * inner_scr[hd]).astype(BF16), v) + qd_scr[hd] * _dot(q, s.astype(BF16))
                kd = (k.astype(F32) * kd_scr[hd]).astype(BF16)
                s_ref[sq, hd] = math.exp(c * lg) * s + lax.dot_general(
                    kd, v, (((0,), (0,)), ((), ())), preferred_element_type=F32)
                o_ref[sq, rows, hd * RET_DV:(hd + 1) * RET_DV] = o.astype(BF16)


def _retention(q, k, v, s0, layer):
    b, l, _ = q.shape
    c = min(RET_CHUNK, l)
    n_chunks = min(RET_CHUNKS_PER_STEP, l // c)
    tl = c * n_chunks
    assert l % tl == 0
    n_seq = RET_SHORT_SEQS if (l == c and b % RET_SHORT_SEQS == 0) else 1
    seq = lambda wd: pl.BlockSpec((n_seq, tl, wd), lambda i, j: (i, j, 0))
    st = pl.BlockSpec((n_seq, RET_HEADS, RET_DK, RET_DV), lambda i, j: (i, 0, 0, 0))
    st_in = pl.BlockSpec((None, n_seq, RET_HEADS, RET_DK, RET_DV), lambda i, j: (layer, i, 0, 0, 0))
    zero_init = s0 is None
    return pl.pallas_call(
        functools.partial(_retention_body, chunk=c, n_chunks=n_chunks, n_seq=n_seq, zero_init=zero_init),
        grid=(b // n_seq, l // tl),
        in_specs=[seq(Q_W), seq(Q_W), seq(V_W)] + ([] if zero_init else [st_in]),
        out_specs=[seq(V_W), st],
        out_shape=[jax.ShapeDtypeStruct((b, l, V_W), BF16),
                   jax.ShapeDtypeStruct((b, RET_HEADS, RET_DK, RET_DV), F32)],
        scratch_shapes=[pltpu.VMEM((RET_HEADS, c, c), F32), pltpu.VMEM((RET_HEADS, c, RET_DV), F32),
                        pltpu.VMEM((RET_HEADS, c, RET_DK), F32)],
        compiler_params=pltpu.CompilerParams(dimension_semantics=("arbitrary", "arbitrary"),
                                             vmem_limit_bytes=VMEM_LIMIT),
        name="retention",
    )(q, k, v, *([] if zero_init else [s0]))


def _slot_permutation():
    sig = np.arange(S5_CHUNK)
    grp = np.arange(S5_GROUPS)
    t_of = SLOTS * (sig[None, :] // SLOTS) + (sig[None, :] % SLOTS - grp[:, None]) % SLOTS
    perm = (t_of[:, :, None] * S5_GROUP + np.arange(S5_GROUP)[None, None, :]).reshape(S5_GROUPS, S5_COLS)
    return (jnp.arange(S5_COLS, dtype=jnp.int32)[None, :, None] == jnp.asarray(perm, jnp.int32)[:, None, :]
            ).astype(BF16)


def _s5_operators(a_re, a_im, log_dt, b_re, b_im, c_re, c_im, d_skip, levels):
    n_pos = S5_CHUNK
    n_l = a_re.shape[0]
    dt = jnp.exp(log_dt)[..., None]
    mag = jnp.exp(a_re * dt)
    lr = mag * jnp.cos(a_im * dt)
    li = mag * jnp.sin(a_im * dt)
    den = a_re * a_re + a_im * a_im
    fr = ((lr - 1.0) * a_re + li * a_im) / den
    fi = (li * a_re - (lr - 1.0) * a_im) / den
    bb_re = fr[..., None] * b_re - fi[..., None] * b_im
    bb_im = fr[..., None] * b_im + fi[..., None] * b_re
    bt_re = bb_re.transpose(0, 1, 3, 2)
    bt_im = bb_im.transpose(0, 1, 3, 2)

    def lam_pow(kk):
        kk = jnp.asarray(kk, F32)[None, None, :, None]
        m = jnp.exp(kk * (a_re * dt)[:, :, None])
        return m * jnp.cos(kk * (a_im * dt)[:, :, None]), m * jnp.sin(kk * (a_im * dt)[:, :, None])

    pr, pi = lam_pow(np.arange(n_pos + 1))
    cp_re = c_re[:, :, None] * pr[:, :, :n_pos, None] - c_im[:, :, None] * pi[:, :, :n_pos, None]
    cp_im = c_re[:, :, None] * pi[:, :, :n_pos, None] + c_im[:, :, None] * pr[:, :, :n_pos, None]
    cp_cat = jnp.concatenate([cp_re, -cp_im], axis=-1).reshape(n_l, S5_GROUPS, S5_COLS, 2 * S5_STATE)
    bt_cat = jnp.concatenate([bt_re, bt_im], axis=-1)
    kcat = jnp.einsum('lgcn,lgxn->lgcx', bt_cat, cp_cat, precision=HIGHEST)
    kpad = jnp.pad(kcat, ((0, 0), (0, 0), (0, 0), (S5_COLS, 0)))
    m_nat = jnp.concatenate(
        [kpad[..., S5_COLS - S5_GROUP * s:2 * S5_COLS - S5_GROUP * s] for s in range(n_pos)], axis=2)
    rr = pr[:, :, n_pos - 1::-1][:, :, :, None]
    ri = pi[:, :, n_pos - 1::-1][:, :, :, None]
    g_re = rr * bt_re[:, :, None] - ri * bt_im[:, :, None]
    g_im = rr * bt_im[:, :, None] + ri * bt_re[:, :, None]
    g_nat = jnp.concatenate([g_re, g_im], axis=-1).reshape(n_l, S5_GROUPS, S5_COLS, 2 * S5_STATE)
    pt_re = pr[:, :, 1:].transpose(0, 1, 3, 2)[..., None]
    pt_im = pi[:, :, 1:].transpose(0, 1, 3, 2)[..., None]
    ct_re = c_re.transpose(0, 1, 3, 2)[:, :, :, None]
    ct_im = c_im.transpose(0, 1, 3, 2)[:, :, :, None]
    hp_re = ct_re * pt_re - ct_im * pt_im
    hp_im = ct_re * pt_im + ct_im * pt_re
    h_nat = jnp.concatenate([hp_re, -hp_im], axis=2).reshape(n_l, S5_GROUPS, 2 * S5_STATE, S5_COLS)
    perm = _slot_permutation()
    m_op = jnp.einsum('gia,lgib->lgab', perm, jnp.einsum('lgij,gjb->lgib', m_nat.astype(BF16), perm))
    g_op = jnp.einsum('gia,lgin->lgan', perm, g_nat.astype(BF16))
    h_op = jnp.einsum('lgnj,gjb->lgnb', h_nat.astype(BF16), perm)
    dr, di = lam_pow(n_pos * (2 ** np.arange(max(levels, 1))))
    d_a = jnp.concatenate([dr, dr], axis=-1)
    d_b = jnp.concatenate([-di, di], axis=-1)
    d_tile = jnp.tile(d_skip.reshape(n_l, S5_GROUPS, 1, S5_GROUP), (1, 1, S5_CHUNK, 1)).reshape(
        n_l, S5_GROUPS, 1, S5_COLS)
    return m_op, g_op, h_op, d_a, d_b, d_tile


def _s5_body(u_ref, x0_ref, m_ref, g_ref, h_ref, da_ref, db_ref, dsk_ref, y_ref, xf_ref, v_scr,
             *, rows, seg, levels):
    r = rows
    half = S5_STATE
    rp = min(S5_FOLD_ROWS, r)
    slot = lax.broadcasted_iota(jnp.int32, (rp, LANES), 1) // S5_GROUP
    masks = [slot == s for s in range(SLOTS)]
    n_col = S5_CHUNK // SLOTS

    for vc in range(S5_LANE_CHUNKS):
        for col in range(n_col):
            for r0 in range(0, r, rp):
                srcs = []
                for tt in range(SLOTS):
                    a = u_ref[vc, pl.ds(col * SLOTS + tt + S5_CHUNK * r0, rp, stride=S5_CHUNK), :]
                    srcs.append(pltpu.roll(a, S5_GROUP * tt, 1) if tt else a)
                for kk in range(SLOTS):
                    acc = srcs[0]
                    for tt in range(1, SLOTS):
                        acc = jnp.where(masks[(kk + tt) % SLOTS], srcs[tt], acc)
                    v_scr[vc * SLOTS + kk, r0:r0 + rp, col * LANES:(col + 1) * LANES] = acc

    row = lax.broadcasted_iota(jnp.int32, (seg, 2 * half), 0)

    def cmul(x, a, b):
        return x * a + pltpu.roll(x, half, 1) * b

    def chunk_scan(w, x0, da, db):
        w = w + jnp.where(row == 0, cmul(x0, da[0:1], db[0:1]), 0.0)
        ws = pltpu.roll(w, half, 1)
        for kk in range(levels):
            s = 1 << kk
            a = da[kk:kk + 1]
            b = db[kk:kk + 1]
            if s % SUBLANES == 0:
                sh, shs = w[:seg - s], ws[:seg - s]
                w, ws = (jnp.concatenate([w[:s], w[s:] + (sh * a + shs * b)], axis=0),
                         jnp.concatenate([ws[:s], ws[s:] + (shs * a - sh * b)], axis=0))
            else:
                sh = jnp.where(row >= s, pltpu.roll(w, s, 0), 0.0)
                shs = jnp.where(row >= s, pltpu.roll(ws, s, 0), 0.0)
                w, ws = w + (sh * a + shs * b), ws + (shs * a - sh * b)
        return jnp.where(row == 0, x0, pltpu.roll(w, 1, 0)), w[seg - 1:seg, :]

    def group_step(g, carry):
        vf = v_scr[g]
        vb = vf.astype(BF16)
        y_intra = _dot(vb, m_ref[g])
        w = _dot(vb, g_ref[g])
        x0 = x0_ref[0, g]
        da = da_ref[g]
        db = db_ref[g]
        if seg == 1:
            xf_ref[0, g] = w + cmul(x0, da[0:1], db[0:1])
            x_in = x0
        else:
            scans = [chunk_scan(w[i * seg:(i + 1) * seg], x0[i:i + 1], da, db) for i in range(r // seg)]
            x_in = jnp.concatenate([sc[0] for sc in scans], axis=0)
            xf_ref[0, g] = jnp.concatenate([sc[1] for sc in scans], axis=0)
        v_scr[g] = y_intra + _dot(x_in.astype(BF16), h_ref[g]) + vf * dsk_ref[g]
        return carry

    lax.fori_loop(0, S5_GROUPS, group_step, 0, unroll=S5_GROUP_UNROLL)

    for vc in range(S5_LANE_CHUNKS):
        for col in range(n_col):
            for r0 in range(0, r, rp):
                ys = [v_scr[vc * SLOTS + kk, r0:r0 + rp, col * LANES:(col + 1) * LANES] for kk in range(SLOTS)]
                for tt in range(SLOTS):
                    acc = ys[(-tt) % SLOTS]
                    for sg in range(1, SLOTS):
                        acc = jnp.where(masks[sg], ys[(sg - tt) % SLOTS], acc)
                    if tt:
                        acc = pltpu.roll(acc, LANES - S5_GROUP * tt, 1)
                    y_ref[vc, pl.ds(col * SLOTS + tt + S5_CHUNK * r0, rp, stride=S5_CHUNK), :] = acc


def _s5(u, x0, ops, layer):
    nb, _, nseq, _ = x0.shape
    n_tok = u.shape[1] // nb
    rows = n_tok // S5_CHUNK
    seg = rows // nseq
    levels = (seg - 1).bit_length()
    assert rows % min(S5_FOLD_ROWS, rows) == 0 and ops[3].shape[2] >= max(levels, 1)
    tok = pl.BlockSpec((S5_LANE_CHUNKS, n_tok, LANES), lambda i: (0, i, 0))
    st = pl.BlockSpec((1, S5_GROUPS, nseq, 2 * S5_STATE), lambda i: (i, 0, 0, 0))
    return pl.pallas_call(
        functools.partial(_s5_body, rows=rows, seg=seg, levels=levels),
        grid=(nb,),
        in_specs=[tok, st] + [_layer_resident(o.shape, layer) for o in ops],
        out_specs=[tok, st],
        out_shape=[jax.ShapeDtypeStruct(u.shape, F32),
                   jax.ShapeDtypeStruct((nb, S5_GROUPS, nseq, 2 * S5_STATE), F32)],
        scratch_shapes=[pltpu.VMEM((S5_GROUPS, rows, S5_COLS), F32)],
        compiler_params=pltpu.CompilerParams(dimension_semantics=("arbitrary",), vmem_limit_bytes=VMEM_LIMIT),
        name="s5",
    )(u, x0, *ops)


def _gelu_tanh(x):
    return 0.5 * x * (1.0 + jnp.tanh(math.sqrt(2.0 / math.pi) * (x + 0.044715 * (x * x * x))))


def _post_body(h_ref, o_ref, g_ref, y_ref, ga_ref, gb_ref, p_ref, gh_ref, wro_ref, wglu_ref, wout_ref, gmlp_ref,
               wup_ref, wdown_ref, gple_ref, wpg_ref, wpp_ref, gfin_ref, out_ref, *, final, sub_rows):
    d = h_ref.shape[1]

    def sub_tile(j, carry):
        rows = pl.ds(pl.multiple_of(j * sub_rows, sub_rows), sub_rows)
        gated = []
        for hd in range(RET_HEADS):
            sl = slice(hd * RET_DV, (hd + 1) * RET_DV)
            gated.append((g_ref[rows, sl].astype(F32)
                          * _rms(o_ref[rows, sl].astype(F32), gh_ref[:, sl])).astype(BF16))
        branch_a = _dot(jnp.concatenate(gated, axis=1), wro_ref[...])
        y = jnp.concatenate([y_ref[vc, rows, :] for vc in range(y_ref.shape[0])], axis=1)
        glu = _dot(_gelu_tanh(y).astype(BF16), wglu_ref[...])
        branch_b = glu[:, :d] * _sigmoid(glu[:, d:])
        merged = ga_ref[rows, :].astype(F32) * branch_a + gb_ref[rows, :].astype(F32) * branch_b
        h = h_ref[rows, :] + _dot(merged.astype(BF16), wout_ref[...])
        inv = _inv_rms(h)
        up = _dot((h * gmlp_ref[...]).astype(BF16), wup_ref[...])
        h = h + (inv * inv) * _dot(jnp.square(jnp.maximum(up, 0.0)).astype(BF16), wdown_ref[...])
        gate = _sigmoid(_inv_rms(h) * _dot((h * gple_ref[...]).astype(BF16), wpg_ref[...]))
        h = h + gate * _dot(p_ref[rows, :].astype(BF16), wpp_ref[...])
        if final:
            h = _rms(h, gfin_ref[...])
        out_ref[rows, :] = h
        return carry

    lax.fori_loop(0, h_ref.shape[0] // sub_rows, sub_tile, 0)


def _post(h, o, g, y, ga, gb, p, layer, w, g_final, final):
    t, d = h.shape
    sub_rows = min(POST_SUB_ROWS, t)
    tm = min(POST_TOKEN_BLOCK, t)
    assert t % tm == 0 and tm % sub_rows == 0
    tok = lambda wd: pl.BlockSpec((tm, wd), lambda i: (i, 0))
    names = ('g_ret_head', 'w_ret_o', 'w_glu', 'w_out', 'g_mlp', 'w_up', 'w_down', 'g_ple', 'w_ple_gate',
             'w_ple_proj')
    return pl.pallas_call(
        functools.partial(_post_body, final=final, sub_rows=sub_rows),
        grid=(t // tm,),
        in_specs=[tok(d), tok(V_W), tok(V_W), pl.BlockSpec((y.shape[0], tm, LANES), lambda i: (0, i, 0)), tok(d),
                  tok(d), pl.BlockSpec((None, tm, p.shape[2]), lambda i: (layer, i, 0))]
                 + [_layer_resident(w[n_].shape, layer) for n_ in names] + [_resident(g_final.shape)],
        out_specs=tok(d),
        out_shape=jax.ShapeDtypeStruct((t, d), F32),
        compiler_params=pltpu.CompilerParams(dimension_semantics=("arbitrary",), vmem_limit_bytes=VMEM_LIMIT),
        name="post",
    )(h, o, g, y, ga, gb, p, *[w[n_] for n_ in names], g_final)


def _rope_tables(pos, n_rows):
    half = RET_DK // 2
    inv = ROPE_BASE ** (-jnp.arange(half, dtype=F32) / half)
    ang = pos.astype(F32)[:, None] * inv[None, :]
    cos = jnp.cos(ang)
    sin = jnp.sin(ang)
    reps = max(1, n_rows // pos.shape[0])
    return (jnp.tile(jnp.concatenate([cos, cos], axis=1), (reps, 1)),
            jnp.tile(jnp.concatenate([-sin, sin], axis=1), (reps, 1)))


def _run_trunk(x, p, pos, s_ret0, s5_re0, s5_im0, w, s5_ops, g_final):
    b, l, d = x.shape
    t = b * l
    depth = w['w_in'].shape[0]
    cos_t, sin_t = _rope_tables(pos, min(TOKEN_BLOCK, t))
    assert l == S5_CHUNK or l % (SUBLANES * S5_CHUNK) == 0
    s5_nseq = b if l == S5_CHUNK else (S5_LONG_SEQS if b % S5_LONG_SEQS == 0 else 1)
    if s5_re0 is None:
        x0_all = jnp.zeros((depth, b, S5_GROUPS, 2 * S5_STATE), F32)
    else:
        x0_all = jnp.concatenate([s5_re0, s5_im0], axis=-1)
    p = p.reshape(depth, t, -1)
    h = x.reshape(t, d)
    rets, s5s = [], []
    for i in range(depth):
        q, k, v, g, u, ga, gb = _in_proj(h, i, w, cos_t, sin_t, l)
        o, s_ret = _retention(q.reshape(b, l, Q_W), k.reshape(b, l, Q_W), v.reshape(b, l, V_W), s_ret0, i)
        x0 = x0_all[i].reshape(b // s5_nseq, s5_nseq, S5_GROUPS, 2 * S5_STATE).transpose(0, 2, 1, 3)
        y, xf = _s5(u, x0, s5_ops, i)
        xf = xf.transpose(0, 2, 1, 3).reshape(b, S5_GROUPS, 2 * S5_STATE)
        h = _post(h, o.reshape(t, V_W), g, y, ga, gb, p, i, w, g_final, final=(i == depth - 1))
        rets.append(s_ret)
        s5s.append(xf)
    s5s = jnp.stack(s5s)
    return h.reshape(b, l, d), jnp.stack(rets), s5s[..., :S5_STATE], s5s[..., S5_STATE:]


def _prepare_weights(g_mix, w_in, g_ret_head, w_ret_o, w_glu_a, w_glu_b, w_out, g_mlp, w_up, w_down, g_ple,
                     w_ple_gate, w_ple_proj):
    row = lambda g: g.reshape(g.shape[0], 1, g.shape[1])
    return {
        'g_mix': row(g_mix), 'w_in': w_in.astype(BF16), 'g_ret_head': row(g_ret_head),
        'w_ret_o': w_ret_o.astype(BF16),
        'w_glu': jnp.concatenate([w_glu_a.astype(BF16), w_glu_b.astype(BF16)], axis=2),
        'w_out': w_out.astype(BF16), 'g_mlp': row(g_mlp), 'w_up': w_up.astype(BF16),
        'w_down': w_down.astype(BF16), 'g_ple': row(g_ple), 'w_ple_gate': w_ple_gate.astype(BF16),
        'w_ple_proj': w_ple_proj.astype(BF16),
    }


def kernel(x_prompt, x_sample, state_ret, state_s5_re, state_s5_im, p_prompt, p_sample, g_mix, w_in, g_ret_head, w_ret_o, s5_a_re, s5_a_im, s5_log_dt, s5_b_re, s5_b_im, s5_c_re, s5_c_im, s5_d, w_glu_a, w_glu_b, w_out, g_mlp, w_up, w_down, g_ple, w_ple_gate, w_ple_proj, g_final):
    d = x_prompt.shape[-1]
    w = _prepare_weights(g_mix, w_in, g_ret_head, w_ret_o, w_glu_a, w_glu_b, w_out, g_mlp, w_up, w_down, g_ple,
                         w_ple_gate, w_ple_proj)
    max_rows = max(x_prompt.shape[1], x_sample.shape[1]) // S5_CHUNK
    s5_ops = _s5_operators(s5_a_re, s5_a_im, s5_log_dt, s5_b_re, s5_b_im, s5_c_re, s5_c_im, s5_d,
                           (max_rows - 1).bit_length())
    g_fin = g_final.reshape(1, d)
    pos_prompt = jnp.arange(x_prompt.shape[1])
    pos_sample = PAST_LEN + jnp.arange(x_sample.shape[1])
    y_p, ret_p, s5re_p, s5im_p = _run_trunk(x_prompt, p_prompt, pos_prompt, None, None, None, w, s5_ops, g_fin)
    y_s, ret_s, s5re_s, s5im_s = _run_trunk(x_sample, p_sample, pos_sample, state_ret, state_s5_re, state_s5_im,
                                            w, s5_ops, g_fin)
    return (y_p, y_s, ret_p, s5re_p, s5im_p, ret_s, s5re_s, s5im_s)
```

```python
import functools
import math

import jax
import jax.numpy as jnp
import numpy as np
from jax import lax
from jax.experimental import pallas as pl
from jax.experimental.pallas import tpu as pltpu

F32 = jnp.float32
BF16 = jnp.bfloat16
HIGHEST = lax.Precision.HIGHEST

EPS = 1e-6
ROPE_BASE = 10000.0
PAST_LEN = 2048
RET_HEADS = 4
RET_DK = 128
RET_DV = 256
Q_W = RET_HEADS * RET_DK
V_W = RET_HEADS * RET_DV
S5_GROUP = 16
S5_GROUPS = 32
S5_WIDTH = S5_GROUP * S5_GROUPS
S5_STATE = 64
S5_CHUNK = 16
S5_COLS = S5_CHUNK * S5_GROUP
LANES = 128
SUBLANES = 8
SLOTS = LANES // S5_GROUP
S5_LANE_CHUNKS = S5_WIDTH // LANES
S5_FOLD_ROWS = 32
S5_LONG_SEQS = 2
S5_GROUP_UNROLL = 4
VMEM_LIMIT = 56 * 1024 * 1024

RET_CHUNK = 256
RET_CHUNKS_PER_STEP = 8
RET_SHORT_SEQS = 8
TOKEN_BLOCK = 512
POST_TOKEN_BLOCK = 512
POST_SUB_ROWS = 256


def _resident(shape):
    nd = len(shape)
    return pl.BlockSpec(shape, lambda *_: (0,) * nd, pipeline_mode=pl.Buffered(1))


def _layer_resident(shape, layer):
    nd = len(shape)
    return pl.BlockSpec((None,) + tuple(shape[1:]), lambda *_: (layer,) + (0,) * (nd - 1),
                        pipeline_mode=pl.Buffered(1))


def _sigmoid(x):
    return 1.0 / (1.0 + jnp.exp(-x))


def _inv_rms(x):
    return lax.rsqrt(jnp.mean(x * x, axis=-1, keepdims=True) + EPS)


def _rms(x, g):
    return (x * _inv_rms(x)) * g


def _dot(a, b):
    return jnp.dot(a, b, preferred_element_type=F32)


def _in_proj_body(x_ref, gm_ref, w_ref, cos_ref, sin_ref, q_ref, k_ref, v_ref, g_ref, u_ref, ga_ref, gb_ref):
    x = x_ref[...]
    inv = _inv_rms(x)
    xg = (x * gm_ref[...]).astype(BF16)
    cos = cos_ref[...]
    sin = sin_ref[...]
    d = x_ref.shape[1]

    def proj(lo, hi):
        return inv * _dot(xg, w_ref[:, lo:hi])

    def rope(z, hd):
        xh = z[:, hd * RET_DK:(hd + 1) * RET_DK]
        return xh * cos + pltpu.roll(xh, RET_DK // 2, 1) * sin

    o_v = 2 * Q_W
    o_u = o_v + 2 * V_W
    o_gate = o_u + S5_WIDTH
    ga_ref[...] = _sigmoid(proj(o_gate, o_gate + d)).astype(BF16)
    gb_ref[...] = _sigmoid(proj(o_gate + d, o_gate + 2 * d)).astype(BF16)
    zg = proj(o_v + V_W, o_v + 2 * V_W)
    g_ref[...] = (zg * _sigmoid(zg)).astype(BF16)
    zq = proj(0, Q_W)
    zk = proj(Q_W, 2 * Q_W)
    for hd in range(RET_HEADS):
        sl = slice(hd * RET_DK, (hd + 1) * RET_DK)
        q_ref[:, sl] = rope(zq, hd).astype(BF16)
        k_ref[:, sl] = (rope(zk, hd) * (RET_DK ** -0.5)).astype(BF16)
    v_ref[...] = proj(o_v, o_v + V_W).astype(BF16)
    zu = proj(o_u, o_u + S5_WIDTH)
    for vc in range(S5_LANE_CHUNKS):
        u_ref[vc] = zu[:, vc * LANES:(vc + 1) * LANES]


def _in_proj(h, layer, w, cos_t, sin_t, seq_len):
    t, d = h.shape
    tm = min(TOKEN_BLOCK, t)
    assert t % tm == 0 and (seq_len % tm == 0 or tm % seq_len == 0)
    n_pos_blocks = cos_t.shape[0] // tm
    tok = lambda wd: pl.BlockSpec((tm, wd), lambda i: (i, 0))
    pos = pl.BlockSpec((tm, RET_DK), lambda i: (i % n_pos_blocks, 0))
    bf = lambda wd: (tok(wd), jax.ShapeDtypeStruct((t, wd), BF16))
    u_out = (pl.BlockSpec((S5_LANE_CHUNKS, tm, LANES), lambda i: (0, i, 0)),
             jax.ShapeDtypeStruct((S5_LANE_CHUNKS, t, LANES), F32))
    outs = (bf(Q_W), bf(Q_W), bf(V_W), bf(V_W), u_out, bf(d), bf(d))
    return pl.pallas_call(
        _in_proj_body,
        grid=(t // tm,),
        in_specs=[tok(d), _layer_resident(w['g_mix'].shape, layer), _layer_resident(w['w_in'].shape, layer),
                  pos, pos],
        out_specs=[o[0] for o in outs],
        out_shape=[o[1] for o in outs],
        compiler_params=pltpu.CompilerParams(dimension_semantics=("arbitrary",), vmem_limit_bytes=VMEM_LIMIT),
        name="in_proj",
    )(h, w['g_mix'], w['w_in'], cos_t, sin_t)


def _retention_body(*refs, chunk, n_chunks, n_seq, zero_init):
    if zero_init:
        q_ref, k_ref, v_ref, o_ref, s_ref, inner_scr, qd_scr, kd_scr = refs
    else:
        q_ref, k_ref, v_ref, s0_ref, o_ref, s_ref, inner_scr, qd_scr, kd_scr = refs
    c = chunk

    @pl.when(pl.program_id(1) == 0)
    def _():
        s_ref[...] = jnp.zeros(s_ref.shape, F32) if zero_init else s0_ref[...]
        row_cc = lax.broadcasted_iota(jnp.int32, (c, c), 0)
        col_cc = lax.broadcasted_iota(jnp.int32, (c, c), 1)
        diff = (row_cc - col_cc).astype(F32)
        idx_v = lax.broadcasted_iota(jnp.int32, (c, RET_DV), 0).astype(F32)
        idx_k = lax.broadcasted_iota(jnp.int32, (c, RET_DK), 0).astype(F32)
        for hd in range(RET_HEADS):
            lg = math.log1p(-(2.0 ** (-5.0 - hd)))
            inner_scr[hd] = jnp.where(diff >= 0.0, jnp.exp(jnp.maximum(diff, 0.0) * lg), 0.0)
            qd_scr[hd] = jnp.exp((idx_v + 1.0) * lg)
            kd_scr[hd] = jnp.exp((c - 1.0 - idx_k) * lg)

    for sq in range(n_seq):
        for ci in range(n_chunks):
            rows = slice(ci * c, (ci + 1) * c)
            for hd in range(RET_HEADS):
                lg = math.log1p(-(2.0 ** (-5.0 - hd)))
                q = q_ref[sq, rows, hd * RET_DK:(hd + 1) * RET_DK]
                k = k_ref[sq, rows, hd * RET_DK:(hd + 1) * RET_DK]
                v = v_ref[sq, rows, hd * RET_DV:(hd + 1) * RET_DV]
                s = s_ref[sq, hd]
                scores = lax.dot_general(q, k, (((1,), (1,)), ((), ())), preferred_element_type=F32)
                o = _dot((scores * inner_scr[hd]).astype(BF16), v) + qd_scr[hd] * _dot(q, s.astype(BF16))
                kd = (k.astype(F32) * kd_scr[hd]).astype(BF16)
                s_ref[sq, hd] = math.exp(c * lg) * s + lax.dot_general(
                    kd, v, (((0,), (0,)), ((), ())), preferred_element_type=F32)
                o_ref[sq, rows, hd * RET_DV:(hd + 1) * RET_DV] = o.astype(BF16)


def _retention(q, k, v, s0, layer):
    b, l, _ = q.shape
    c = min(RET_CHUNK, l)
    n_chunks = min(RET_CHUNKS_PER_STEP, l // c)
    tl = c * n_chunks
    assert l % tl == 0
    n_seq = RET_SHORT_SEQS if (l == c and b % RET_SHORT_SEQS == 0) else 1
    seq = lambda wd: pl.BlockSpec((n_seq, tl, wd), lambda i, j: (i, j, 0))
    st = pl.BlockSpec((n_seq, RET_HEADS, RET_DK, RET_DV), lambda i, j: (i, 0, 0, 0))
    st_in = pl.BlockSpec((None, n_seq, RET_HEADS, RET_DK, RET_DV), lambda i, j: (layer, i, 0, 0, 0))
    zero_init = s0 is None
    return pl.pallas_call(
        functools.partial(_retention_body, chunk=c, n_chunks=n_chunks, n_seq=n_seq, zero_init=zero_init),
        grid=(b // n_seq, l // tl),
        in_specs=[seq(Q_W), seq(Q_W), seq(V_W)] + ([] if zero_init else [st_in]),
        out_specs=[seq(V_W), st],
        out_shape=[jax.ShapeDtypeStruct((b, l, V_W), BF16),
                   jax.ShapeDtypeStruct((b, RET_HEADS, RET_DK, RET_DV), F32)],
        scratch_shapes=[pltpu.VMEM((RET_HEADS, c, c), F32), pltpu.VMEM((RET_HEADS, c, RET_DV), F32),
                        pltpu.VMEM((RET_HEADS, c, RET_DK), F32)],
        compiler_params=pltpu.CompilerParams(dimension_semantics=("arbitrary", "arbitrary"),
                                             vmem_limit_bytes=VMEM_LIMIT),
        name="retention",
    )(q, k, v, *([] if zero_init else [s0]))


def _slot_permutation():
    sig = np.arange(S5_CHUNK)
    grp = np.arange(S5_GROUPS)
    t_of = SLOTS * (sig[None, :] // SLOTS) + (sig[None, :] % SLOTS - grp[:, None]) % SLOTS
    perm = (t_of[:, :, None] * S5_GROUP + np.arange(S5_GROUP)[None, None, :]).reshape(S5_GROUPS, S5_COLS)
    return (jnp.arange(S5_COLS, dtype=jnp.int32)[None, :, None] == jnp.asarray(perm, jnp.int32)[:, None, :]
            ).astype(BF16)


def _s5_operators(a_re, a_im, log_dt, b_re, b_im, c_re, c_im, d_skip, levels):
    n_pos = S5_CHUNK
    n_l = a_re.shape[0]
    dt = jnp.exp(log_dt)[..., None]
    mag = jnp.exp(a_re * dt)
    lr = mag * jnp.cos(a_im * dt)
    li = mag * jnp.sin(a_im * dt)
    den = a_re * a_re + a_im * a_im
    fr = ((lr - 1.0) * a_re + li * a_im) / den
    fi = (li * a_re - (lr - 1.0) * a_im) / den
    bb_re = fr[..., None] * b_re - fi[..., None] * b_im
    bb_im = fr[..., None] * b_im + fi[..., None] * b_re
    bt_re = bb_re.transpose(0, 1, 3, 2)
    bt_im = bb_im.transpose(0, 1, 3, 2)

    def lam_pow(kk):
        kk = jnp.asarray(kk, F32)[None, None, :, None]
        m = jnp.exp(kk * (a_re * dt)[:, :, None])
        return m * jnp.cos(kk * (a_im * dt)[:, :, None]), m * jnp.sin(kk * (a_im * dt)[:, :, None])

    pr, pi = lam_pow(np.arange(n_pos + 1))
    cp_re = c_re[:, :, None] * pr[:, :, :n_pos, None] - c_im[:, :, None] * pi[:, :, :n_pos, None]
    cp_im = c_re[:, :, None] * pi[:, :, :n_pos, None] + c_im[:, :, None] * pr[:, :, :n_pos, None]
    cp_cat = jnp.concatenate([cp_re, -cp_im], axis=-1).reshape(n_l, S5_GROUPS, S5_COLS, 2 * S5_STATE)
    bt_cat = jnp.concatenate([bt_re, bt_im], axis=-1)
    kcat = jnp.einsum('lgcn,lgxn->lgcx', bt_cat, cp_cat, precision=HIGHEST)
    kpad = jnp.pad(kcat, ((0, 0), (0, 0), (0, 0), (S5_COLS, 0)))
    m_nat = jnp.concatenate(
        [kpad[..., S5_COLS - S5_GROUP * s:2 * S5_COLS - S5_GROUP * s] for s in range(n_pos)], axis=2)
    rr = pr[:, :, n_pos - 1::-1][:, :, :, None]
    ri = pi[:, :, n_pos - 1::-1][:, :, :, None]
    g_re = rr * bt_re[:, :, None] - ri * bt_im[:, :, None]
    g_im = rr * bt_im[:, :, None] + ri * bt_re[:, :, None]
    g_nat = jnp.concatenate([g_re, g_im], axis=-1).reshape(n_l, S5_GROUPS, S5_COLS, 2 * S5_STATE)
    pt_re = pr[:, :, 1:].transpose(0, 1, 3, 2)[..., None]
    pt_im = pi[:, :, 1:].transpose(0, 1, 3, 2)[..., None]
    ct_re = c_re.transpose(0, 1, 3, 2)[:, :, :, None]
    ct_im = c_im.transpose(0, 1, 3, 2)[:, :, :, None]
    hp_re = ct_re * pt_re - ct_im * pt_im
    hp_im = ct_re * pt_im + ct_im * pt_re
    h_nat = jnp.concatenate([hp_re, -hp_im], axis=2).reshape(n_l, S5_GROUPS, 2 * S5_STATE, S5_COLS)
    perm = _slot_permutation()
    m_op = jnp.einsum('gia,lgib->lgab', perm, jnp.einsum('lgij,gjb->lgib', m_nat.astype(BF16), perm))
    g_op = jnp.einsum('gia,lgin->lgan', perm, g_nat.astype(BF16))
    h_op = jnp.einsum('lgnj,gjb->lgnb', h_nat.astype(BF16), perm)
    dr, di = lam_pow(n_pos * (2 ** np.arange(max(levels, 1))))
    d_a = jnp.concatenate([dr, dr], axis=-1)
    d_b = jnp.concatenate([-di, di], axis=-1)
    d_tile = jnp.tile(d_skip.reshape(n_l, S5_GROUPS, 1, S5_GROUP), (1, 1, S5_CHUNK, 1)).reshape(
        n_l, S5_GROUPS, 1, S5_COLS)
    return m_op, g_op, h_op, d_a, d_b, d_tile


def _s5_body(u_ref, x0_ref, m_ref, g_ref, h_ref, da_ref, db_ref, dsk_ref, y_ref, xf_ref, v_scr,
             *, rows, seg, levels):
    r = rows
    half = S5_STATE
    rp = min(S5_FOLD_ROWS, r)
    slot = lax.broadcasted_iota(jnp.int32, (rp, LANES), 1) // S5_GROUP
    masks = [slot == s for s in range(SLOTS)]
    n_col = S5_CHUNK // SLOTS

    for vc in range(S5_LANE_CHUNKS):
        for col in range(n_col):
            for r0 in range(0, r, rp):
                srcs = []
                for tt in range(SLOTS):
                    a = u_ref[vc, pl.ds(col * SLOTS + tt + S5_CHUNK * r0, rp, stride=S5_CHUNK), :]
                    srcs.append(pltpu.roll(a, S5_GROUP * tt, 1) if tt else a)
                for kk in range(SLOTS):
                    acc = srcs[0]
                    for tt in range(1, SLOTS):
                        acc = jnp.where(masks[(kk + tt) % SLOTS], srcs[tt], acc)
                    v_scr[vc * SLOTS + kk, r0:r0 + rp, col * LANES:(col + 1) * LANES] = acc

    row = lax.broadcasted_iota(jnp.int32, (seg, 2 * half), 0)

    def cmul(x, a, b):
        return x * a + pltpu.roll(x, half, 1) * b

    def chunk_scan(w, x0, da, db):
        w = w + jnp.where(row == 0, cmul(x0, da[0:1], db[0:1]), 0.0)
        ws = pltpu.roll(w, half, 1)
        for kk in range(levels):
            s = 1 << kk
            a = da[kk:kk + 1]
            b = db[kk:kk + 1]
            if s % SUBLANES == 0:
                sh, shs = w[:seg - s], ws[:seg - s]
                w, ws = (jnp.concatenate([w[:s], w[s:] + (sh * a + shs * b)], axis=0),
                         jnp.concatenate([ws[:s], ws[s:] + (shs * a - sh * b)], axis=0))
            else:
                sh = jnp.where(row >= s, pltpu.roll(w, s, 0), 0.0)
                shs = jnp.where(row >= s, pltpu.roll(ws, s, 0), 0.0)
                w, ws = w + (sh * a + shs * b), ws + (shs * a - sh * b)
        return jnp.where(row == 0, x0, pltpu.roll(w, 1, 0)), w[seg - 1:seg, :]

    def group_step(g, carry):
        vf = v_scr[g]
        vb = vf.astype(BF16)
        y_intra = _dot(vb, m_ref[g])
        w = _dot(vb, g_ref[g])
        x0 = x0_ref[0, g]
        da = da_ref[g]
        db = db_ref[g]
        if seg == 1:
            xf_ref[0, g] = w + cmul(x0, da[0:1], db[0:1])
            x_in = x0
        else:
            scans = [chunk_scan(w[i * seg:(i + 1) * seg], x0[i:i + 1], da, db) for i in range(r // seg)]
            x_in = jnp.concatenate([sc[0] for sc in scans], axis=0)
            xf_ref[0, g] = jnp.concatenate([sc[1] for sc in scans], axis=0)
        v_scr[g] = y_intra + _dot(x_in.astype(BF16), h_ref[g]) + vf * dsk_ref[g]
        return carry

    lax.fori_loop(0, S5_GROUPS, group_step, 0, unroll=S5_GROUP_UNROLL)

    for vc in range(S5_LANE_CHUNKS):
        for col in range(n_col):
            for r0 in range(0, r, rp):
                ys = [v_scr[vc * SLOTS + kk, r0:r0 + rp, col * LANES:(col + 1) * LANES] for kk in range(SLOTS)]
                for tt in range(SLOTS):
                    acc = ys[(-tt) % SLOTS]
                    for sg in range(1, SLOTS):
                        acc = jnp.where(masks[sg], ys[(sg - tt) % SLOTS], acc)
                    if tt:
                        acc = pltpu.roll(acc, LANES - S5_GROUP * tt, 1)
                    y_ref[vc, pl.ds(col * SLOTS + tt + S5_CHUNK * r0, rp, stride=S5_CHUNK), :] = acc


def _s5(u, x0, ops, layer):
    nb, _, nseq, _ = x0.shape
    n_tok = u.shape[1] // nb
    rows = n_tok // S5_CHUNK
    seg = rows // nseq
    levels = (seg - 1).bit_length()
    assert rows % min(S5_FOLD_ROWS, rows) == 0 and ops[3].shape[2] >= max(levels, 1)
    tok = pl.BlockSpec((S5_LANE_CHUNKS, n_tok, LANES), lambda i: (0, i, 0))
    st = pl.BlockSpec((1, S5_GROUPS, nseq, 2 * S5_STATE), lambda i: (i, 0, 0, 0))
    return pl.pallas_call(
        functools.partial(_s5_body, rows=rows, seg=seg, levels=levels),
        grid=(nb,),
        in_specs=[tok, st] + [_layer_resident(o.shape, layer) for o in ops],
        out_specs=[tok, st],
        out_shape=[jax.ShapeDtypeStruct(u.shape, F32),
                   jax.ShapeDtypeStruct((nb, S5_GROUPS, nseq, 2 * S5_STATE), F32)],
        scratch_shapes=[pltpu.VMEM((S5_GROUPS, rows, S5_COLS), F32)],
        compiler_params=pltpu.CompilerParams(dimension_semantics=("arbitrary",), vmem_limit_bytes=VMEM_LIMIT),
        name="s5",
    )(u, x0, *ops)


def _gelu_tanh(x):
    return 0.5 * x * (1.0 + jnp.tanh(math.sqrt(2.0 / math.pi) * (x + 0.044715 * (x * x * x))))


def _post_body(h_ref, o_ref, g_ref, y_ref, ga_ref, gb_ref, p_ref, gh_ref, wro_ref, wglu_ref, wout_ref, gmlp_ref,
               wup_ref, wdown_ref, gple_ref, wpg_ref, wpp_ref, gfin_ref, out_ref, *, final, sub_rows):
    d = h_ref.shape[1]

    def sub_tile(j, carry):
        rows = pl.ds(pl.multiple_of(j * sub_rows, sub_rows), sub_rows)
        gated = []
        for hd in range(RET_HEADS):
            sl = slice(hd * RET_DV, (hd + 1) * RET_DV)
            gated.append((g_ref[rows, sl].astype(F32)
                          * _rms(o_ref[rows, sl].astype(F32), gh_ref[:, sl])).astype(BF16))
        branch_a = _dot(jnp.concatenate(gated, axis=1), wro_ref[...])
        y = jnp.concatenate([y_ref[vc, rows, :] for vc in range(y_ref.shape[0])], axis=1)
        glu = _dot(_gelu_tanh(y).astype(BF16), wglu_ref[...])
        branch_b = glu[:, :d] * _sigmoid(glu[:, d:])
        merged = ga_ref[rows, :].astype(F32) * branch_a + gb_ref[rows, :].astype(F32) * branch_b
        h = h_ref[rows, :] + _dot(merged.astype(BF16), wout_ref[...])
        inv = _inv_rms(h)
        up = _dot((h * gmlp_ref[...]).astype(BF16), wup_ref[...])
        h = h + (inv * inv) * _dot(jnp.square(jnp.maximum(up, 0.0)).astype(BF16), wdown_ref[...])
        gate = _sigmoid(_inv_rms(h) * _dot((h * gple_ref[...]).astype(BF16), wpg_ref[...]))
        h = h + gate * _dot(p_ref[rows, :].astype(BF16), wpp_ref[...])
        if final:
            h = _rms(h, gfin_ref[...])
        out_ref[rows, :] = h
        return carry

    lax.fori_loop(0, h_ref.shape[0] // sub_rows, sub_tile, 0)


def _post(h, o, g, y, ga, gb, p, layer, w, g_final, final):
    t, d = h.shape
    sub_rows = min(POST_SUB_ROWS, t)
    tm = min(POST_TOKEN_BLOCK, t)
    assert t % tm == 0 and tm % sub_rows == 0
    tok = lambda wd: pl.BlockSpec((tm, wd), lambda i: (i, 0))
    names = ('g_ret_head', 'w_ret_o', 'w_glu', 'w_out', 'g_mlp', 'w_up', 'w_down', 'g_ple', 'w_ple_gate',
             'w_ple_proj')
    return pl.pallas_call(
        functools.partial(_post_body, final=final, sub_rows=sub_rows),
        grid=(t // tm,),
        in_specs=[tok(d), tok(V_W), tok(V_W), pl.BlockSpec((y.shape[0], tm, LANES), lambda i: (0, i, 0)), tok(d),
                  tok(d), pl.BlockSpec((None, tm, p.shape[2]), lambda i: (layer, i, 0))]
                 + [_layer_resident(w[n_].shape, layer) for n_ in names] + [_resident(g_final.shape)],
        out_specs=tok(d),
        out_shape=jax.ShapeDtypeStruct((t, d), F32),
        compiler_params=pltpu.CompilerParams(dimension_semantics=("arbitrary",), vmem_limit_bytes=VMEM_LIMIT),
        name="post",
    )(h, o, g, y, ga, gb, p, *[w[n_] for n_ in names], g_final)


def _rope_tables(pos, n_rows):
    half = RET_DK // 2
    inv = ROPE_BASE ** (-jnp.arange(half, dtype=F32) / half)
    ang = pos.astype(F32)[:, None] * inv[None, :]
    cos = jnp.cos(ang)
    sin = jnp.sin(ang)
    reps = max(1, n_rows // pos.shape[0])
    return (jnp.tile(jnp.concatenate([cos, cos], axis=1), (reps, 1)),
            jnp.tile(jnp.concatenate([-sin, sin], axis=1), (reps, 1)))


def _run_trunk(x, p, pos, s_ret0, s5_re0, s5_im0, w, s5_ops, g_final):
    b, l, d = x.shape
    t = b * l
    depth = w['w_in'].shape[0]
    cos_t, sin_t = _rope_tables(pos, min(TOKEN_BLOCK, t))
    assert l == S5_CHUNK or l % (SUBLANES * S5_CHUNK) == 0
    s5_nseq = b if l == S5_CHUNK else (S5_LONG_SEQS if b % S5_LONG_SEQS == 0 else 1)
    if s5_re0 is None:
        x0_all = jnp.zeros((depth, b, S5_GROUPS, 2 * S5_STATE), F32)
    else:
        x0_all = jnp.concatenate([s5_re0, s5_im0], axis=-1)
    p = p.reshape(depth, t, -1)
    h = x.reshape(t, d)
    rets, s5s = [], []
    for i in range(depth):
        q, k, v, g, u, ga, gb = _in_proj(h, i, w, cos_t, sin_t, l)
        o, s_ret = _retention(q.reshape(b, l, Q_W), k.reshape(b, l, Q_W), v.reshape(b, l, V_W), s_ret0, i)
        x0 = x0_all[i].reshape(b // s5_nseq, s5_nseq, S5_GROUPS, 2 * S5_STATE).transpose(0, 2, 1, 3)
        y, xf = _s5(u, x0, s5_ops, i)
        xf = xf.transpose(0, 2, 1, 3).reshape(b, S5_GROUPS, 2 * S5_STATE)
        h = _post(h, o.reshape(t, V_W), g, y, ga, gb, p, i, w, g_final, final=(i == depth - 1))
        rets.append(s_ret)
        s5s.append(xf)
    s5s = jnp.stack(s5s)
    return h.reshape(b, l, d), jnp.stack(rets), s5s[..., :S5_STATE], s5s[..., S5_STATE:]


def _prepare_weights(g_mix, w_in, g_ret_head, w_ret_o, w_glu_a, w_glu_b, w_out, g_mlp, w_up, w_down, g_ple,
                     w_ple_gate, w_ple_proj):
    row = lambda g: g.reshape(g.shape[0], 1, g.shape[1])
    return {
        'g_mix': row(g_mix), 'w_in': w_in.astype(BF16), 'g_ret_head': row(g_ret_head),
        'w_ret_o': w_ret_o.astype(BF16),
        'w_glu': jnp.concatenate([w_glu_a.astype(BF16), w_glu_b.astype(BF16)], axis=2),
        'w_out': w_out.astype(BF16), 'g_mlp': row(g_mlp), 'w_up': w_up.astype(BF16),
        'w_down': w_down.astype(BF16), 'g_ple': row(g_ple), 'w_ple_gate': w_ple_gate.astype(BF16),
        'w_ple_proj': w_ple_proj.astype(BF16),
    }


def kernel(x_prompt, x_sample, state_ret, state_s5_re, state_s5_im, p_prompt, p_sample, g_mix, w_in, g_ret_head, w_ret_o, s5_a_re, s5_a_im, s5_log_dt, s5_b_re, s5_b_im, s5_c_re, s5_c_im, s5_d, w_glu_a, w_glu_b, w_out, g_mlp, w_up, w_down, g_ple, w_ple_gate, w_ple_proj, g_final):
    d = x_prompt.shape[-1]
    w = _prepare_weights(g_mix, w_in, g_ret_head, w_ret_o, w_glu_a, w_glu_b, w_out, g_mlp, w_up, w_down, g_ple,
                         w_ple_gate, w_ple_proj)
    max_rows = max(x_prompt.shape[1], x_sample.shape[1]) // S5_CHUNK
    s5_ops = _s5_operators(s5_a_re, s5_a_im, s5_log_dt, s5_b_re, s5_b_im, s5_c_re, s5_c_im, s5_d,
                           (max_rows - 1).bit_length())
    g_fin = g_final.reshape(1, d)
    pos_prompt = jnp.arange(x_prompt.shape[1])
    pos_sample = PAST_LEN + jnp.arange(x_sample.shape[1])
    y_p, ret_p, s5re_p, s5im_p = _run_trunk(x_prompt, p_prompt, pos_prompt, None, None, None, w, s5_ops, g_fin)
    y_s, ret_s, s5re_s, s5im_s = _run_trunk(x_sample, p_sample, pos_sample, state_ret, state_s5_re, state_s5_im,
                                            w, s5_ops, g_fin)
    return (y_p, y_s, ret_p, s5re_p, s5im_p, ret_s, s5re_s, s5im_s)
```
